```python
import math
import jax, jax.numpy as jnp
from jax import lax
import numpy as np

D_MODEL = 1024
BATCH = 32
SEQ = 2048
DEPTH = 4

N_MIXERS = 2
N_MLA_LAYERS = (DEPTH + 1) // 2
N_MLSTM_LAYERS = DEPTH // 2

MLA_HEADS = 16
MLA_Q_RANK = 512
MLA_KV_RANK = 256
MLA_NOPE_DIM = 64
MLA_ROPE_DIM = 32
MLA_V_DIM = 64
MLA_QK_DIM = MLA_NOPE_DIM + MLA_ROPE_DIM
MLA_IN_DIM = MLA_Q_RANK + MLA_KV_RANK + MLA_ROPE_DIM
ROPE_THETA = 10000.0
ATTN_BLOCK = 128

MLSTM_HEADS = 8
MLSTM_QK_DIM = 64
MLSTM_V_DIM = 128
MLSTM_CONV = 4
MLSTM_CHUNK = 64
GATE_SOFTCAP = 15.0
MLSTM_QK_WIDTH = 2 * MLSTM_HEADS * MLSTM_QK_DIM
MLSTM_HV = MLSTM_HEADS * MLSTM_V_DIM
MLSTM_IN_DIM = MLSTM_QK_WIDTH + 2 * MLSTM_HV + 2 * MLSTM_HEADS

MLP_HIDDEN = 4 * D_MODEL
EPS = 1e-6

kernel_name = "mla_mlstm_interleaved_hybrid"


def rms_norm(x, g):
    xf = x.astype(jnp.float32)
    y = xf * lax.rsqrt(jnp.mean(xf * xf, axis=-1, keepdims=True) + EPS)
    return (y * g.astype(jnp.float32)).astype(x.dtype)


def softcap(x, cap):
    return cap * jnp.tanh(x / cap)


def rope_tables(positions, dim):
    inv = ROPE_THETA ** (-jnp.arange(0, dim, 2, dtype=jnp.float32) / dim)
    ang = positions.astype(jnp.float32)[..., None] * inv
    return jnp.cos(ang), jnp.sin(ang)


def apply_rope(x, cos, sin):
    xf = x.astype(jnp.float32)
    half = xf.shape[-1] // 2
    x1, x2 = xf[..., :half], xf[..., half:]
    out = jnp.concatenate([x1 * cos - x2 * sin, x1 * sin + x2 * cos], axis=-1)
    return out.astype(x.dtype)


def mla_mixer(u, cos, sin, w_in, g_q, g_kv, w_uq, w_ukv, w_o):
    B, S, _ = u.shape
    proj = u @ w_in
    c_q = proj[..., :MLA_Q_RANK]
    c_kv = proj[..., MLA_Q_RANK:MLA_Q_RANK + MLA_KV_RANK]
    k_r = proj[..., MLA_Q_RANK + MLA_KV_RANK:]
    q = (rms_norm(c_q, g_q) @ w_uq).reshape(B, S, MLA_HEADS, MLA_QK_DIM)
    q_n = q[..., :MLA_NOPE_DIM]
    q_r = apply_rope(q[..., MLA_NOPE_DIM:], cos[:, :, None, :], sin[:, :, None, :])
    kv = (rms_norm(c_kv, g_kv) @ w_ukv).reshape(B, S, MLA_HEADS, MLA_NOPE_DIM + MLA_V_DIM)
    k_n = kv[..., :MLA_NOPE_DIM]
    v = kv[..., MLA_NOPE_DIM:]
    k_r = apply_rope(k_r, cos, sin)
    scale = MLA_QK_DIM ** -0.5
    outs = []
    for blk in range(S // ATTN_BLOCK):
        q0 = blk * ATTN_BLOCK
        k_end = q0 + ATTN_BLOCK
        s = (jnp.einsum('bqhd,bkhd->bhqk', q_n[:, q0:k_end], k_n[:, :k_end])
             + jnp.einsum('bqhr,bkr->bhqk', q_r[:, q0:k_end], k_r[:, :k_end])).astype(jnp.float32) * scale
        q_idx = q0 + jnp.arange(ATTN_BLOCK)
        k_idx = jnp.arange(k_end)
        s = jnp.where(k_idx[None, :] <= q_idx[:, None], s, -jnp.inf)
        p = jax.nn.softmax(s, axis=-1).astype(v.dtype)
        outs.append(jnp.einsum('bhqk,bkhd->bqhd', p, v[:, :k_end]))
    o = jnp.concatenate(outs, axis=1).reshape(B, S, MLA_HEADS * MLA_V_DIM)
    return o @ w_o


def causal_depthwise_conv(x, w, b):
    C = x.shape[-1]
    y = lax.conv_general_dilated(x, w[:, None, :].astype(x.dtype), window_strides=(1,),
                                 padding=[(w.shape[0] - 1, 0)],
                                 dimension_numbers=('NWC', 'WIO', 'NWC'),
                                 feature_group_count=C)
    return y + b.astype(x.dtype)


def mlstm_chunkwise(q, k, v, ig, lf):
    B, S, H, DK = q.shape
    DV = v.shape[-1]
    L = MLSTM_CHUNK
    NC = S // L

    def to_chunks(a):
        a = a.astype(jnp.float32).reshape((B, NC, L, H) + a.shape[3:])
        return jnp.moveaxis(jnp.moveaxis(a, 1, 0), 3, 2)

    causal = jnp.tril(jnp.ones((L, L), dtype=bool))

    def step(carry, xs):
        C, n, m = carry
        qc, kc, vc, ic, fc = xs
        b = jnp.cumsum(fc, axis=-1)
        d = jnp.where(causal, b[..., :, None] - b[..., None, :] + ic[..., None, :], -jnp.inf)
        inter = b + m[..., None]
        m_t = jnp.maximum(inter, jnp.max(d, axis=-1))
        w_intra = jnp.exp(d - m_t[..., None])
        a_inter = jnp.exp(inter - m_t)
        s = jnp.einsum('bhjd,bhsd->bhjs', qc, kc) * w_intra
        num = (a_inter[..., None] * jnp.einsum('bhvd,bhjd->bhjv', C, qc)
               + jnp.einsum('bhjs,bhsv->bhjv', s, vc))
        den = a_inter * jnp.einsum('bhd,bhjd->bhj', n, qc) + jnp.sum(s, axis=-1)
        h = num / jnp.maximum(jnp.abs(den), jnp.exp(-m_t))[..., None]
        b_end = b[..., -1]
        g = b_end[..., None] - b + ic
        m_new = jnp.maximum(b_end + m, jnp.max(g, axis=-1))
        a_st = jnp.exp(b_end + m - m_new)
        w_st = jnp.exp(g - m_new[..., None])
        C_new = a_st[..., None, None] * C + jnp.einsum('bhs,bhsv,bhsd->bhvd', w_st, vc, kc)
        n_new = a_st[..., None] * n + jnp.einsum('bhs,bhsd->bhd', w_st, kc)
        return (C_new, n_new, m_new), h

    init = (jnp.zeros((B, H, DV, DK), jnp.float32),
            jnp.zeros((B, H, DK), jnp.float32),
            jnp.zeros((B, H), jnp.float32))
    _, hs = lax.scan(step, init, (to_chunks(q), to_chunks(k), to_chunks(v), to_chunks(ig), to_chunks(lf)))
    hs = jnp.moveaxis(jnp.moveaxis(hs, 2, 3), 0, 1)
    return hs.reshape(B, S, H, DV)


def mlstm_mixer(u, w_in, conv_w, conv_b, i_b, f_b, head_g, w_o):
    B, S, _ = u.shape
    H = MLSTM_HEADS
    proj = u @ w_in
    qk_pre = proj[..., :MLSTM_QK_WIDTH]
    v = proj[..., MLSTM_QK_WIDTH:MLSTM_QK_WIDTH + MLSTM_HV].reshape(B, S, H, MLSTM_V_DIM)
    o_pre = proj[..., MLSTM_QK_WIDTH + MLSTM_HV:MLSTM_QK_WIDTH + 2 * MLSTM_HV]
    i_pre = proj[..., MLSTM_QK_WIDTH + 2 * MLSTM_HV:MLSTM_QK_WIDTH + 2 * MLSTM_HV + H]
    f_pre = proj[..., MLSTM_QK_WIDTH + 2 * MLSTM_HV + H:]
    qk = jax.nn.silu(causal_depthwise_conv(qk_pre, conv_w, conv_b))
    q = qk[..., :H * MLSTM_QK_DIM].reshape(B, S, H, MLSTM_QK_DIM)
    k = qk[..., H * MLSTM_QK_DIM:].reshape(B, S, H, MLSTM_QK_DIM) * (MLSTM_QK_DIM ** -0.5)
    ig = softcap((i_pre + i_b).astype(jnp.float32), GATE_SOFTCAP)
    lf = jax.nn.log_sigmoid(softcap((f_pre + f_b).astype(jnp.float32), GATE_SOFTCAP))
    h = mlstm_chunkwise(q, k, v, ig, lf)
    h = rms_norm(h, head_g.reshape(H, MLSTM_V_DIM)).astype(u.dtype).reshape(B, S, MLSTM_HV)
    return (jax.nn.sigmoid(o_pre) * h) @ w_o


def squared_relu_mlp(u, w1, w2):
    a = jnp.maximum(u @ w1, 0)
    return (a * a) @ w2


def setup_inputs(seed: int = 0) -> dict:
    key = jax.random.key(seed)
    ks = iter(jax.random.split(key, 32))
    f32 = jnp.float32

    def w(shape, fan_in):
        return jax.random.normal(next(ks), shape, f32) * (fan_in ** -0.5)

    def gain(shape):
        return 1.0 + 0.02 * jax.random.normal(next(ks), shape, f32)

    x = jax.random.normal(next(ks), (BATCH, SEQ, D_MODEL), f32)
    offsets = jax.random.randint(next(ks), (BATCH, 1), 0, 4096, dtype=jnp.int32)
    positions = (offsets + jnp.arange(SEQ, dtype=jnp.int32)[None, :]).astype(jnp.int32)
    NA, NM = N_MLA_LAYERS, N_MLSTM_LAYERS
    f_bias = (jnp.linspace(3.0, 6.0, MLSTM_HEADS, dtype=f32)[None, :]
              + 0.1 * jax.random.normal(next(ks), (NM, MLSTM_HEADS), f32))
    return {
        "x": x,
        "positions": positions,
        "norm_mix_g": gain((DEPTH, D_MODEL)),
        "norm_mlp_g": gain((DEPTH, D_MODEL)),
        "final_g": gain((D_MODEL,)),
        "mla_w_in": w((NA, D_MODEL, MLA_IN_DIM), D_MODEL),
        "mla_q_norm_g": gain((NA, MLA_Q_RANK)),
        "mla_kv_norm_g": gain((NA, MLA_KV_RANK)),
        "mla_w_uq": w((NA, MLA_Q_RANK, MLA_HEADS * MLA_QK_DIM), MLA_Q_RANK),
        "mla_w_ukv": w((NA, MLA_KV_RANK, MLA_HEADS * (MLA_NOPE_DIM + MLA_V_DIM)), MLA_KV_RANK),
        "mla_w_o": w((NA, MLA_HEADS * MLA_V_DIM, D_MODEL), MLA_HEADS * MLA_V_DIM),
        "mlstm_w_in": w((NM, D_MODEL, MLSTM_IN_DIM), D_MODEL),
        "mlstm_conv_w": w((NM, MLSTM_CONV, MLSTM_QK_WIDTH), MLSTM_CONV),
        "mlstm_conv_b": 0.02 * jax.random.normal(next(ks), (NM, MLSTM_QK_WIDTH), f32),
        "mlstm_i_b": 0.1 * jax.random.normal(next(ks), (NM, MLSTM_HEADS), f32),
        "mlstm_f_b": f_bias,
        "mlstm_head_norm_g": gain((NM, MLSTM_HV)),
        "mlstm_w_o": w((NM, MLSTM_HV, D_MODEL), MLSTM_HV),
        "mlp_w1": w((DEPTH, D_MODEL, MLP_HIDDEN), D_MODEL),
        "mlp_w2": w((DEPTH, MLP_HIDDEN, D_MODEL), MLP_HIDDEN),
    }


def reference(x, positions, norm_mix_g, norm_mlp_g, final_g,
              mla_w_in, mla_q_norm_g, mla_kv_norm_g, mla_w_uq, mla_w_ukv, mla_w_o,
              mlstm_w_in, mlstm_conv_w, mlstm_conv_b, mlstm_i_b, mlstm_f_b, mlstm_head_norm_g, mlstm_w_o,
              mlp_w1, mlp_w2):
    cos, sin = rope_tables(positions, MLA_ROPE_DIM)
    h = x
    for layer in range(DEPTH):
        u = rms_norm(h, norm_mix_g[layer])
        j = layer // N_MIXERS
        if layer % N_MIXERS == 0:
            h = h + mla_mixer(u, cos, sin, mla_w_in[j], mla_q_norm_g[j], mla_kv_norm_g[j],
                              mla_w_uq[j], mla_w_ukv[j], mla_w_o[j])
        else:
            h = h + mlstm_mixer(u, mlstm_w_in[j], mlstm_conv_w[j], mlstm_conv_b[j], mlstm_i_b[j],
                                mlstm_f_b[j], mlstm_head_norm_g[j], mlstm_w_o[j])
        u = rms_norm(h, norm_mlp_g[layer])
        h = h + squared_relu_mlp(u, mlp_w1[layer], mlp_w2[layer])
    return rms_norm(h, final_g)
```

```python
import functools
import math

import jax
import jax.numpy as jnp
from jax import lax
from jax.experimental import pallas as pl
from jax.experimental.pallas import tpu as pltpu

EPS = 1e-6
ROPE_THETA = 10000.0
GATE_SOFTCAP = 15.0

MLA_HEADS = 16
MLA_NOPE = 64
MLA_ROPE = 32
MLA_V = 64
MLA_QK = MLA_NOPE + MLA_ROPE
MLA_Q_RANK = 512
MLA_KV_RANK = 256

MLSTM_HEADS = 8
MLSTM_DK = 64
MLSTM_DV = 128
MLSTM_CONV = 4

LANES = 128
HEAD_BLOCK = LANES
CONV_HALO = 16
VMEM_LIMIT = 56 * 1024 * 1024

BF16 = jnp.bfloat16
F32 = jnp.float32


def _rms(x, g):
    return x * lax.rsqrt(jnp.mean(x * x, axis=-1, keepdims=True) + EPS) * g


def _dot(a, b):
    return jnp.dot(a, b, preferred_element_type=F32)


def _params(*sem):
    return pltpu.CompilerParams(dimension_semantics=sem, vmem_limit_bytes=VMEM_LIMIT)


def _const_spec(shape):
    return pl.BlockSpec(shape, lambda *_: (0,) * len(shape), pipeline_mode=pl.Buffered(1))


def _rope_kernel(pos_ref, inv_ref, cos_ref, sin_ref):
    ang = pos_ref[...].astype(F32) * inv_ref[...]
    cos_ref[...] = jnp.cos(ang)
    sin_ref[...] = jnp.sin(ang)


def _rope_tables(pos_col, inv_pat, tr):
    m = pos_col.shape[0]
    return pl.pallas_call(
        _rope_kernel,
        out_shape=(jax.ShapeDtypeStruct((m, LANES), F32),) * 2,
        grid=(m // tr,),
        in_specs=[pl.BlockSpec((tr, 1), lambda i: (i, 0)), _const_spec((1, LANES))],
        out_specs=(pl.BlockSpec((tr, LANES), lambda i: (i, 0)),) * 2,
        compiler_params=_params("parallel"),
        name="rope_tables",
    )(pos_col, inv_pat)


def _mla_pre_kernel(h_ref, g_ref, win_ref, gq_ref, gkv_ref, wuq_ref, wuk_ref, wuv_ref,
                    cos_ref, sin_ref, q_ref, k_ref, v_ref, *, q_scale):
    u = _rms(h_ref[...], g_ref[...]).astype(BF16)
    proj = _dot(u, win_ref[...])
    cq = _rms(proj[:, :MLA_Q_RANK], gq_ref[...]).astype(BF16)
    ckv = _rms(proj[:, MLA_Q_RANK:MLA_Q_RANK + MLA_KV_RANK], gkv_ref[...]).astype(BF16)
    kr = proj[:, MLA_Q_RANK + MLA_KV_RANK:]
    c = cos_ref[...]
    s = sin_ref[...]
    half = HEAD_BLOCK // 2
    kr = kr * c + pltpu.roll(kr, half, axis=1) * s
    q = _dot(cq, wuq_ref[...])
    kn = _dot(ckv, wuk_ref[...])
    cs = c * q_scale
    ss = s * q_scale
    for h in range(MLA_HEADS):
        sl = slice(h * HEAD_BLOCK, (h + 1) * HEAD_BLOCK)
        qh = q[:, sl]
        q_ref[:, sl] = (qh * cs + pltpu.roll(qh, half, axis=1) * ss).astype(BF16)
        k_ref[:, sl] = (kn[:, sl] + kr).astype(BF16)
    v_ref[...] = _dot(ckv, wuv_ref[...]).astype(BF16)


def _mla_pre(h, g, win, gq, gkv, wuq, wuk, wuv, cosp, sinp, tm, q_scale):
    m, d = h.shape
    hq = MLA_HEADS * HEAD_BLOCK
    hv = MLA_HEADS * MLA_V
    row = lambda i: (i, 0)
    return pl.pallas_call(
        functools.partial(_mla_pre_kernel, q_scale=q_scale),
        out_shape=(jax.ShapeDtypeStruct((m, hq), BF16), jax.ShapeDtypeStruct((m, hq), BF16),
                   jax.ShapeDtypeStruct((m, hv), BF16)),
        grid=(m // tm,),
        in_specs=[pl.BlockSpec((tm, d), row), _const_spec(g.shape), _const_spec(win.shape),
                  _const_spec(gq.shape), _const_spec(gkv.shape), _const_spec(wuq.shape),
                  _const_spec(wuk.shape), _const_spec(wuv.shape),
                  pl.BlockSpec((tm, LANES), row), pl.BlockSpec((tm, LANES), row)],
        out_specs=(pl.BlockSpec((tm, hq), row), pl.BlockSpec((tm, hq), row), pl.BlockSpec((tm, hv), row)),
        compiler_params=_params("parallel"),
        name="mla_pre",
    )(h, g, win, gq, gkv, wuq, wuk, wuv, cosp, sinp)


def _attn_kernel(q_ref, k_ref, v_ref, o_ref, *, tq):
    seq = q_ref.shape[0]
    nq = seq // tq
    lane = lax.broadcasted_iota(jnp.int32, (tq, LANES), 1)
    head_lanes = [lane < MLA_V, lane >= MLA_V]
    row = lax.broadcasted_iota(jnp.int32, (tq, tq), 0)
    col = lax.broadcasted_iota(jnp.int32, (tq, tq), 1)
    causal = row >= col

    def q_tile(qi, carry):
        q0 = pl.multiple_of(qi * tq, tq)
        qs = [q_ref[pl.ds(q0, tq), hh * HEAD_BLOCK:(hh + 1) * HEAD_BLOCK] for hh in range(2)]

        def step(k0, state, masked):
            vb = v_ref[pl.ds(k0, tq), :]
            new = []
            for hh in range(2):
                m, l, acc = state[hh]
                kb = k_ref[pl.ds(k0, tq), hh * HEAD_BLOCK:(hh + 1) * HEAD_BLOCK]
                s = lax.dot_general(qs[hh], kb, (((1,), (1,)), ((), ())), preferred_element_type=F32)
                if masked:
                    s = jnp.where(causal, s, -jnp.inf)
                m_new = jnp.maximum(m, jnp.max(s, axis=-1, keepdims=True))
                alpha = jnp.exp2(m - m_new)
                p = jnp.exp2(s - m_new)
                l = alpha * l + jnp.sum(p, axis=-1, keepdims=True)
                vm = jnp.where(head_lanes[hh], vb, jnp.zeros_like(vb))
                acc = alpha * acc + _dot(p.astype(BF16), vm)
                new.append((m_new, l, acc))
            return tuple(new)

        init = tuple((jnp.full((tq, 1), -jnp.inf, F32), jnp.zeros((tq, 1), F32),
                      jnp.zeros((tq, LANES), F32)) for _ in range(2))
        state = lax.fori_loop(0, qi, lambda j, st: step(pl.multiple_of(j * tq, tq), st, False), init)
        state = step(q0, state, True)
        out = state[0][2] * (1.0 / state[0][1]) + state[1][2] * (1.0 / state[1][1])
        o_ref[pl.ds(q0, tq), :] = out.astype(o_ref.dtype)
        return carry

    lax.fori_loop(0, nq, q_tile, 0)


def _attention(q, k, v, batch, seq, tq):
    m = q.shape[0]
    pairs = MLA_HEADS // 2
    blk = lambda b, p: (b, p)
    return pl.pallas_call(
        functools.partial(_attn_kernel, tq=tq),
        out_shape=jax.ShapeDtypeStruct((m, MLA_HEADS * MLA_V), BF16),
        grid=(batch, pairs),
        in_specs=[pl.BlockSpec((seq, 2 * HEAD_BLOCK), blk), pl.BlockSpec((seq, 2 * HEAD_BLOCK), blk),
                  pl.BlockSpec((seq, 2 * MLA_V), blk)],
        out_specs=pl.BlockSpec((seq, 2 * MLA_V), blk),
        compiler_params=_params("parallel", "parallel"),
        name="mla_attention",
    )(q, k, v)


def _post_kernel(a_ref, wo_ref, h_ref, g_ref, w1_ref, w2_ref, gf_ref, o_ref, *, hidden_chunk, final):
    h1 = h_ref[...] + _dot(a_ref[...], wo_ref[...])
    u = _rms(h1, g_ref[...]).astype(BF16)
    acc = h1
    hidden = w1_ref.shape[1]
    for c in range(hidden // hidden_chunk):
        sl = slice(c * hidden_chunk, (c + 1) * hidden_chunk)
        a = jnp.maximum(_dot(u, w1_ref[:, sl]), 0.0)
        acc = acc + _dot((a * a).astype(BF16), w2_ref[sl, :])
    if final:
        acc = _rms(acc, gf_ref[...])
    o_ref[...] = acc


def _post(a, wo, h, g, w1, w2, gf, tm, final):
    m, d = h.shape
    row = lambda i: (i, 0)
    return pl.pallas_call(
        functools.partial(_post_kernel, hidden_chunk=1024, final=final),
        out_shape=jax.ShapeDtypeStruct((m, d), F32),
        grid=(m // tm,),
        in_specs=[pl.BlockSpec((tm, a.shape[1]), row), _const_spec(wo.shape), pl.BlockSpec((tm, d), row),
                  _const_spec(g.shape), _const_spec(w1.shape), _const_spec(w2.shape), _const_spec(gf.shape)],
        out_specs=pl.BlockSpec((tm, d), row),
        compiler_params=_params("parallel"),
        name="post_mlp",
    )(a, wo, h, g, w1, w2, gf)


def _mlstm_pre_kernel(h_ref, halo_ref, g_ref, wqk_ref, wv_ref, wog_ref, wg_ref, cw_ref, cb_ref, gb_ref,
                      q_ref, kt_ref, v_ref, og_ref, ig_ref, lf_ref, *, tiles_per_seq):
    tm = h_ref.shape[0]
    g = g_ref[...]
    u = _rms(h_ref[...], g).astype(BF16)
    first = pl.program_id(0) % tiles_per_seq == 0
    uh = jnp.where(first, 0.0, _rms(halo_ref[...], g)).astype(BF16)
    qk = _dot(jnp.concatenate([uh, u], axis=0), wqk_ref[...])
    cw = cw_ref[...]
    y = cb_ref[...]
    for j in range(MLSTM_CONV):
        off = CONV_HALO - (MLSTM_CONV - 1) + j
        y = y + qk[off:off + tm, :] * cw[j:j + 1, :]
    y = y * (1.0 / (1.0 + jnp.exp(-y)))
    hq = MLSTM_HEADS * MLSTM_DK
    q_ref[...] = y[:, :hq].astype(BF16)
    kt_ref[...] = (y[:, hq:] * (MLSTM_DK ** -0.5)).T.astype(BF16)
    v_ref[...] = _dot(u, wv_ref[...]).astype(BF16)
    o_pre = _dot(u, wog_ref[...])
    og_ref[...] = (1.0 / (1.0 + jnp.exp(-o_pre))).astype(BF16)
    gates = lax.dot_general(wg_ref[...], u, (((1,), (1,)), ((), ())), preferred_element_type=F32) + gb_ref[...]
    gates = GATE_SOFTCAP * jnp.tanh(gates * (1.0 / GATE_SOFTCAP))
    ig_ref[...] = gates[:MLSTM_HEADS]
    z = gates[MLSTM_HEADS:]
    lf_ref[...] = -(jnp.maximum(-z, 0.0) + jnp.log1p(jnp.exp(-jnp.abs(z))))


def _mlstm_pre(h, g, wqk, wv, wog, wg, cw, cb, gb, tm, seq):
    m, d = h.shape
    hq = MLSTM_HEADS * MLSTM_DK
    hv = MLSTM_HEADS * MLSTM_DV
    row = lambda i: (i, 0)
    colblk = lambda i: (0, i)
    halo = lambda i: (jnp.maximum(i * (tm // CONV_HALO) - 1, 0), 0)
    return pl.pallas_call(
        functools.partial(_mlstm_pre_kernel, tiles_per_seq=seq // tm),
        out_shape=(jax.ShapeDtypeStruct((m, hq), BF16), jax.ShapeDtypeStruct((hq, m), BF16),
                   jax.ShapeDtypeStruct((m, hv), BF16), jax.ShapeDtypeStruct((m, hv), BF16),
                   jax.ShapeDtypeStruct((MLSTM_HEADS, m), F32), jax.ShapeDtypeStruct((MLSTM_HEADS, m), F32)),
        grid=(m // tm,),
        in_specs=[pl.BlockSpec((tm, d), row), pl.BlockSpec((CONV_HALO, d), halo), _const_spec(g.shape),
                  _const_spec(wqk.shape), _const_spec(wv.shape), _const_spec(wog.shape), _const_spec(wg.shape),
                  _const_spec(cw.shape), _const_spec(cb.shape), _const_spec(gb.shape)],
        out_specs=(pl.BlockSpec((tm, hq), row), pl.BlockSpec((hq, tm), colblk), pl.BlockSpec((tm, hv), row),
                   pl.BlockSpec((tm, hv), row), pl.BlockSpec((MLSTM_HEADS, tm), colblk),
                   pl.BlockSpec((MLSTM_HEADS, tm), colblk)),
        compiler_params=_params("parallel"),
        name="mlstm_pre",
    )(h, h, g, wqk, wv, wog, wg, cw, cb, gb)


def _split3(x):
    x1 = x.astype(BF16)
    r1 = x - x1.astype(F32)
    x2 = r1.astype(BF16)
    x3 = (r1 - x2.astype(F32)).astype(BF16)
    return x1, x2, x3


def _mlstm_chunk_kernel(q_ref, kt_ref, v_ref, og_ref, ig_ref, lf_ref, hg_ref, o_ref, ct_ref, m_ref):
    lc = q_ref.shape[0]
    pair = 2 * MLSTM_DK

    @pl.when(pl.program_id(1) == 0)
    def _():
        ct_ref[...] = jnp.zeros_like(ct_ref)
        m_ref[...] = jnp.zeros_like(m_ref)

    row = lax.broadcasted_iota(jnp.int32, (lc, lc), 0)
    col = lax.broadcasted_iota(jnp.int32, (lc, lc), 1)
    causal = row >= col
    upper = jnp.where(row <= col, 1.0, 0.0).astype(BF16)
    f = lf_ref[...]
    f1, f2, f3 = _split3(f)
    b = _dot(f1, upper) + _dot(f2, upper) + _dot(f3, upper)
    a = ig_ref[...] - b
    m_prev = m_ref[:, 0:1]
    m_last = jnp.maximum(m_prev, jnp.max(a, axis=1, keepdims=True))
    m_new = b[:, lc - 1:lc] + m_last
    a_st = jnp.exp(m_prev - m_last)
    w_st = jnp.exp(a - m_last)
    m_ref[...] = jnp.broadcast_to(m_new, m_ref.shape)

    lane = lax.broadcasted_iota(jnp.int32, (lc, pair), 1)
    ones_col = jnp.where(lane == 0, 1.0, 0.0).astype(BF16)
    srow = lax.broadcasted_iota(jnp.int32, (pair, 1), 0)
    hg = hg_ref[...]

    for p in range(MLSTM_HEADS // 2):
        qb = q_ref[:, p * pair:(p + 1) * pair]
        ktb = kt_ref[p * pair:(p + 1) * pair, :]
        ct = ct_ref[p]
        ct_b = ct.astype(BF16)
        upd = []
        for hh in range(2):
            h = 2 * p + hh
            qm = jnp.where((lane < MLSTM_DK) if hh == 0 else (lane >= MLSTM_DK), qb, jnp.zeros_like(qb))
            a_h = a[h:h + 1, :]
            d = jnp.where(causal, a_h, -jnp.inf)
            m_col = jnp.maximum(m_prev[h:h + 1, :], jnp.max(d, axis=1, keepdims=True))
            b_col = jnp.sum(jnp.where(causal, f[h:h + 1, :], 0.0), axis=1, keepdims=True)
            s = _dot(qm, ktb) * jnp.exp(d - m_col)
            vsl = slice(h * MLSTM_DV, (h + 1) * MLSTM_DV)
            vaug = jnp.concatenate([v_ref[:, vsl], ones_col], axis=1)
            nd = jnp.exp(m_prev[h:h + 1, :] - m_col) * _dot(qm, ct_b) + _dot(s.astype(BF16), vaug)
            den = jnp.maximum(jnp.abs(nd[:, MLSTM_DV:MLSTM_DV + 1]), jnp.exp(-(b_col + m_col)))
            hv = nd[:, :MLSTM_DV] / den
            hn = _rms(hv, hg[:, vsl])
            o_ref[:, vsl] = (og_ref[:, vsl].astype(F32) * hn).astype(o_ref.dtype)
            kw = (ktb.astype(F32) * w_st[h:h + 1, :]).astype(BF16)
            upd.append(_dot(kw, vaug))
        top = srow < MLSTM_DK
        decay = jnp.where(top, a_st[2 * p:2 * p + 1, :], a_st[2 * p + 1:2 * p + 2, :])
        ct_ref[p] = decay * ct + jnp.where(top, upd[0], upd[1])


def _mlstm_chunk(q, kt, v, og, ig, lf, hg, batch, seq, lc):
    m = q.shape[0]
    hq = MLSTM_HEADS * MLSTM_DK
    hv = MLSTM_HEADS * MLSTM_DV
    nc = seq // lc
    row = lambda b, c: (b * nc + c, 0)
    colblk = lambda b, c: (0, b * nc + c)
    return pl.pallas_call(
        _mlstm_chunk_kernel,
        out_shape=jax.ShapeDtypeStruct((m, hv), BF16),
        grid=(batch, nc),
        in_specs=[pl.BlockSpec((lc, hq), row), pl.BlockSpec((hq, lc), colblk), pl.BlockSpec((lc, hv), row),
                  pl.BlockSpec((lc, hv), row), pl.BlockSpec((MLSTM_HEADS, lc), colblk),
                  pl.BlockSpec((MLSTM_HEADS, lc), colblk), _const_spec(hg.shape)],
        out_specs=pl.BlockSpec((lc, hv), row),
        scratch_shapes=[pltpu.VMEM((MLSTM_HEADS // 2, 2 * MLSTM_DK, 2 * MLSTM_DV), F32),
                        pltpu.VMEM((MLSTM_HEADS, LANES), F32)],
        compiler_params=_params("parallel", "arbitrary"),
        name="mlstm_chunk",
    )(q, kt, v, og, ig, lf, hg)


def _head_block(nope, x1, x2):
    z = jnp.zeros(nope.shape[:-1] + (HEAD_BLOCK - MLA_NOPE - MLA_ROPE,), nope.dtype)
    return jnp.concatenate([nope[..., :48], x1, nope[..., 48:], z, x2], axis=-1)


def _mla_weights(w_in, w_uq, w_ukv):
    d = w_in.shape[0]
    half = MLA_ROPE // 2
    kr = w_in[:, MLA_Q_RANK + MLA_KV_RANK:]
    kr_blk = _head_block(jnp.zeros((d, MLA_NOPE), w_in.dtype), kr[:, :half], kr[:, half:])
    win = jnp.concatenate([w_in[:, :MLA_Q_RANK + MLA_KV_RANK], kr_blk], axis=1).astype(BF16)
    uq = w_uq.reshape(MLA_Q_RANK, MLA_HEADS, MLA_QK)
    wuq = _head_block(uq[..., :MLA_NOPE], uq[..., MLA_NOPE:MLA_NOPE + half], uq[..., MLA_NOPE + half:])
    wuq = wuq.reshape(MLA_Q_RANK, MLA_HEADS * HEAD_BLOCK).astype(BF16)
    ukv = w_ukv.reshape(MLA_KV_RANK, MLA_HEADS, MLA_NOPE + MLA_V)
    zr = jnp.zeros((MLA_KV_RANK, MLA_HEADS, half), w_ukv.dtype)
    wuk = _head_block(ukv[..., :MLA_NOPE], zr, zr).reshape(MLA_KV_RANK, MLA_HEADS * HEAD_BLOCK).astype(BF16)
    wuv = ukv[..., MLA_NOPE:].reshape(MLA_KV_RANK, MLA_HEADS * MLA_V).astype(BF16)
    return win, wuq, wuk, wuv


def _rope_inv_pattern():
    inv = ROPE_THETA ** (-jnp.arange(0, MLA_ROPE, 2, dtype=F32) / MLA_ROPE)
    z48 = jnp.zeros((48,), F32)
    return jnp.concatenate([z48, -inv, z48, inv]).reshape(1, LANES)


def kernel(x, positions, norm_mix_g, norm_mlp_g, final_g, mla_w_in, mla_q_norm_g, mla_kv_norm_g, mla_w_uq,
           mla_w_ukv, mla_w_o, mlstm_w_in, mlstm_conv_w, mlstm_conv_b, mlstm_i_b, mlstm_f_b,
           mlstm_head_norm_g, mlstm_w_o, mlp_w1, mlp_w2):
    batch, seq, d = x.shape
    m = batch * seq
    depth = norm_mix_g.shape[0]
    tm = min(512, seq)
    tq = min(256, seq)
    lc = min(256, seq)
    assert seq % tm == 0 and seq % tq == 0 and seq % lc == 0 and tm % CONV_HALO == 0

    h = x.reshape(m, d)
    cosp, sinp = _rope_tables(positions.reshape(m, 1), _rope_inv_pattern(), min(2048, m))
    q_scale = (MLA_QK ** -0.5) * math.log2(math.e)
    row = lambda v: v.reshape(1, -1)
    hq = MLSTM_HEADS * MLSTM_DK
    hv = MLSTM_HEADS * MLSTM_DV

    for layer in range(depth):
        j = layer // 2
        g_mix = row(norm_mix_g[layer])
        if layer % 2 == 0:
            win, wuq, wuk, wuv = _mla_weights(mla_w_in[j], mla_w_uq[j], mla_w_ukv[j])
            q, k, v = _mla_pre(h, g_mix, win, row(mla_q_norm_g[j]), row(mla_kv_norm_g[j]), wuq, wuk, wuv,
                               cosp, sinp, tm, q_scale)
            mixed = _attention(q, k, v, batch, seq, tq)
            w_o = mla_w_o[j]
        else:
            w = mlstm_w_in[j]
            wqk = w[:, :2 * hq].astype(BF16)
            wv = w[:, 2 * hq:2 * hq + hv].astype(BF16)
            wog = w[:, 2 * hq + hv:2 * hq + 2 * hv].astype(BF16)
            wg = w[:, 2 * hq + 2 * hv:].T.astype(BF16)
            gb = jnp.concatenate([mlstm_i_b[j], mlstm_f_b[j]]).reshape(2 * MLSTM_HEADS, 1)
            q, kt, v, og, ig, lf = _mlstm_pre(h, g_mix, wqk, wv, wog, wg, mlstm_conv_w[j],
                                              row(mlstm_conv_b[j]), gb, tm, seq)
            mixed = _mlstm_chunk(q, kt, v, og, ig, lf, row(mlstm_head_norm_g[j]), batch, seq, lc)
            w_o = mlstm_w_o[j]
        h = _post(mixed, w_o.astype(BF16), h, row(norm_mlp_g[layer]), mlp_w1[layer].astype(BF16),
                  mlp_w2[layer].astype(BF16), row(final_g), tm, final=(layer == depth - 1))
    return h.reshape(batch, seq, d)
```

```python
import functools
import math

import jax
import jax.numpy as jnp
from jax import lax
from jax.experimental import pallas as pl
from jax.experimental.pallas import tpu as pltpu

EPS = 1e-6
ROPE_THETA = 10000.0
GATE_SOFTCAP = 15.0

MLA_HEADS = 16
MLA_NOPE = 64
MLA_ROPE = 32
MLA_V = 64
MLA_QK = MLA_NOPE + MLA_ROPE
MLA_Q_RANK = 512
MLA_KV_RANK = 256

MLSTM_HEADS = 8
MLSTM_DK = 64
MLSTM_DV = 128
MLSTM_CONV = 4

LANES = 128
HEAD_BLOCK = LANES
KV_BLOCK = LANES
CONV_HALO = 16
VMEM_LIMIT = 56 * 1024 * 1024

BF16 = jnp.bfloat16
F32 = jnp.float32


def _rms(x, g):
    return x * lax.rsqrt(jnp.mean(x * x, axis=-1, keepdims=True) + EPS) * g


def _dot(a, b):
    return jnp.dot(a, b, preferred_element_type=F32)


def _params(*sem):
    return pltpu.CompilerParams(dimension_semantics=sem, vmem_limit_bytes=VMEM_LIMIT)


def _const_spec(shape):
    return pl.BlockSpec(shape, lambda *_: (0,) * len(shape), pipeline_mode=pl.Buffered(1))


def _rope_kernel(pos_ref, inv_ref, cos_ref, sin_ref):
    ang = pos_ref[...].astype(F32) * inv_ref[...]
    cos_ref[...] = jnp.cos(ang)
    sin_ref[...] = jnp.sin(ang)


def _rope_tables(pos_col, inv_pat, tr):
    m = pos_col.shape[0]
    return pl.pallas_call(
        _rope_kernel,
        out_shape=(jax.ShapeDtypeStruct((m, LANES), F32),) * 2,
        grid=(m // tr,),
        in_specs=[pl.BlockSpec((tr, 1), lambda i: (i, 0)), _const_spec((1, LANES))],
        out_specs=(pl.BlockSpec((tr, LANES), lambda i: (i, 0)),) * 2,
        compiler_params=_params("parallel"),
        name="rope_tables",
    )(pos_col, inv_pat)


def _mla_pre_kernel(h_ref, g_ref, win_ref, gq_ref, gkv_ref, wuq_ref, wuk_ref, wuv_ref,
                    cos_ref, sin_ref, q_ref, kt_ref, v_ref, *, q_scale):
    u = _rms(h_ref[...], g_ref[...]).astype(BF16)
    proj = _dot(u, win_ref[...])
    cq = _rms(proj[:, :MLA_Q_RANK], gq_ref[...]).astype(BF16)
    ckv = _rms(proj[:, MLA_Q_RANK:MLA_Q_RANK + MLA_KV_RANK], gkv_ref[...]).astype(BF16)
    kr = proj[:, MLA_Q_RANK + MLA_KV_RANK:]
    c = cos_ref[...]
    s = sin_ref[...]
    half = HEAD_BLOCK // 2
    kr = kr * c + pltpu.roll(kr, half, axis=1) * s
    q = _dot(cq, wuq_ref[...])
    kn = _dot(ckv, wuk_ref[...])
    cs = c * q_scale
    ss = s * q_scale
    for h in range(MLA_HEADS):
        sl = slice(h * HEAD_BLOCK, (h + 1) * HEAD_BLOCK)
        qh = q[:, sl]
        q_ref[:, sl] = (qh * cs + pltpu.roll(qh, half, axis=1) * ss).astype(BF16)
        kt_ref[sl, :] = (kn[:, sl] + kr).T.astype(BF16)
    v_ref[...] = _dot(ckv, wuv_ref[...]).astype(BF16)


def _mla_pre(h, g, win, gq, gkv, wuq, wuk, wuv, cosp, sinp, tm, q_scale):
    m, d = h.shape
    hq = MLA_HEADS * HEAD_BLOCK
    hv = MLA_HEADS * MLA_V
    row = lambda i: (i, 0)
    return pl.pallas_call(
        functools.partial(_mla_pre_kernel, q_scale=q_scale),
        out_shape=(jax.ShapeDtypeStruct((m, hq), BF16), jax.ShapeDtypeStruct((hq, m), BF16),
                   jax.ShapeDtypeStruct((m, hv), BF16)),
        grid=(m // tm,),
        in_specs=[pl.BlockSpec((tm, d), row), _const_spec(g.shape), _const_spec(win.shape),
                  _const_spec(gq.shape), _const_spec(gkv.shape), _const_spec(wuq.shape),
                  _const_spec(wuk.shape), _const_spec(wuv.shape),
                  pl.BlockSpec((tm, LANES), row), pl.BlockSpec((tm, LANES), row)],
        out_specs=(pl.BlockSpec((tm, hq), row), pl.BlockSpec((hq, tm), lambda i: (0, i)),
                   pl.BlockSpec((tm, hv), row)),
        compiler_params=_params("parallel"),
        name="mla_pre",
    )(h, g, win, gq, gkv, wuq, wuk, wuv, cosp, sinp)


def _attn_kernel(q_ref, kt_ref, v_ref, o_ref, kbd_ref, vbd_ref, *, tq):
    seq = q_ref.shape[0]
    nkb = seq // KV_BLOCK
    pw = 2 * KV_BLOCK

    @pl.when((pl.program_id(0) == 0) & (pl.program_id(1) == 0))
    def _():
        kbd_ref[...] = jnp.zeros_like(kbd_ref)

    lane_v = lax.broadcasted_iota(jnp.int32, (KV_BLOCK, 2 * MLA_V), 1)
    for j in range(nkb):
        ks = slice(j * KV_BLOCK, (j + 1) * KV_BLOCK)
        kbd_ref[:HEAD_BLOCK, j * pw:j * pw + KV_BLOCK] = kt_ref[:HEAD_BLOCK, ks]
        kbd_ref[HEAD_BLOCK:, j * pw + KV_BLOCK:(j + 1) * pw] = kt_ref[HEAD_BLOCK:, ks]
        vb = v_ref[ks, :]
        vbd_ref[j * pw:j * pw + KV_BLOCK, :] = jnp.where(lane_v < MLA_V, vb, jnp.zeros_like(vb))
        vbd_ref[j * pw + KV_BLOCK:(j + 1) * pw, :] = jnp.where(lane_v >= MLA_V, vb, jnp.zeros_like(vb))

    rel = (lax.broadcasted_iota(jnp.int32, (tq, KV_BLOCK), 0)
           - lax.broadcasted_iota(jnp.int32, (tq, KV_BLOCK), 1))
    lane_o = lax.broadcasted_iota(jnp.int32, (tq, 2 * MLA_V), 1)
    for qi in range(seq // tq):
        q0 = qi * tq
        nb = (q0 + tq) // KV_BLOCK
        s = _dot(q_ref[q0:q0 + tq, :], kbd_ref[:, :nb * pw])
        chunks = []
        for j in range(nb):
            for hh in range(2):
                c = s[:, (2 * j + hh) * KV_BLOCK:(2 * j + hh + 1) * KV_BLOCK]
                if (j + 1) * KV_BLOCK - 1 > q0:
                    c = jnp.where(rel >= j * KV_BLOCK - q0, c, -jnp.inf)
                chunks.append(c)
        m = [jnp.max(functools.reduce(jnp.maximum, chunks[hh::2]), axis=-1, keepdims=True) for hh in range(2)]
        lsum = [None, None]
        p = []
        for idx, c in enumerate(chunks):
            e = jnp.exp2(c - m[idx % 2])
            lsum[idx % 2] = e if lsum[idx % 2] is None else lsum[idx % 2] + e
            p.append(e.astype(BF16))
        o = _dot(jnp.concatenate(p, axis=1), vbd_ref[:nb * pw, :])
        inv = [1.0 / jnp.sum(l, axis=-1, keepdims=True) for l in lsum]
        o_ref[q0:q0 + tq, :] = (o * jnp.where(lane_o < MLA_V, inv[0], inv[1])).astype(o_ref.dtype)


def _attention(q, kt, v, batch, seq, tq):
    m = q.shape[0]
    pairs = MLA_HEADS // 2
    nkb = seq // KV_BLOCK
    blk = lambda b, p: (b, p)
    return pl.pallas_call(
        functools.partial(_attn_kernel, tq=tq),
        out_shape=jax.ShapeDtypeStruct((m, MLA_HEADS * MLA_V), BF16),
        grid=(batch, pairs),
        in_specs=[pl.BlockSpec((seq, 2 * HEAD_BLOCK), blk),
                  pl.BlockSpec((2 * HEAD_BLOCK, seq), lambda b, p: (p, b)),
                  pl.BlockSpec((seq, 2 * MLA_V), blk)],
        out_specs=pl.BlockSpec((seq, 2 * MLA_V), blk),
        scratch_shapes=[pltpu.VMEM((2 * HEAD_BLOCK, nkb * 2 * KV_BLOCK), BF16),
                        pltpu.VMEM((nkb * 2 * KV_BLOCK, 2 * MLA_V), BF16)],
        compiler_params=_params("arbitrary", "arbitrary"),
        name="mla_attention",
    )(q, kt, v)


def _post_kernel(a_ref, wo_ref, h_ref, g_ref, w1_ref, w2_ref, gf_ref, o_ref, *, hidden_chunk, final):
    h1 = h_ref[...] + _dot(a_ref[...], wo_ref[...])
    u = _rms(h1, g_ref[...]).astype(BF16)
    acc = h1
    hidden = w1_ref.shape[1]
    for c in range(hidden // hidden_chunk):
        sl = slice(c * hidden_chunk, (c + 1) * hidden_chunk)
        a = jnp.maximum(_dot(u, w1_ref[:, sl]), 0.0)
        acc = acc + _dot((a * a).astype(BF16), w2_ref[sl, :])
    if final:
        acc = _rms(acc, gf_ref[...])
    o_ref[...] = acc


def _post(a, wo, h, g, w1, w2, gf, tm, final):
    m, d = h.shape
    row = lambda i: (i, 0)
    return pl.pallas_call(
        functools.partial(_post_kernel, hidden_chunk=1024, final=final),
        out_shape=jax.ShapeDtypeStruct((m, d), F32),
        grid=(m // tm,),
        in_specs=[pl.BlockSpec((tm, a.shape[1]), row), _const_spec(wo.shape), pl.BlockSpec((tm, d), row),
                  _const_spec(g.shape), _const_spec(w1.shape), _const_spec(w2.shape), _const_spec(gf.shape)],
        out_specs=pl.BlockSpec((tm, d), row),
        compiler_params=_params("parallel"),
        name="post_mlp",
    )(a, wo, h, g, w1, w2, gf)


def _mlstm_pre_kernel(h_ref, halo_ref, g_ref, wqk_ref, wv_ref, wog_ref, wg_ref, cw_ref, cb_ref, gb_ref,
                      q_ref, kt_ref, v_ref, og_ref, ig_ref, lf_ref, *, tiles_per_seq):
    tm = h_ref.shape[0]
    g = g_ref[...]
    u = _rms(h_ref[...], g).astype(BF16)
    first = pl.program_id(0) % tiles_per_seq == 0
    uh = jnp.where(first, 0.0, _rms(halo_ref[...], g)).astype(BF16)
    qk = _dot(jnp.concatenate([uh, u], axis=0), wqk_ref[...])
    cw = cw_ref[...]
    y = cb_ref[...]
    for j in range(MLSTM_CONV):
        off = CONV_HALO - (MLSTM_CONV - 1) + j
        y = y + qk[off:off + tm, :] * cw[j:j + 1, :]
    y = y * (1.0 / (1.0 + jnp.exp(-y)))
    hq = MLSTM_HEADS * MLSTM_DK
    q_ref[...] = y[:, :hq].astype(BF16)
    kt_ref[...] = (y[:, hq:] * (MLSTM_DK ** -0.5)).T.astype(BF16)
    v_ref[...] = _dot(u, wv_ref[...]).astype(BF16)
    o_pre = _dot(u, wog_ref[...])
    og_ref[...] = (1.0 / (1.0 + jnp.exp(-o_pre))).astype(BF16)
    gates = lax.dot_general(wg_ref[...], u, (((1,), (1,)), ((), ())), preferred_element_type=F32) + gb_ref[...]
    gates = GATE_SOFTCAP * jnp.tanh(gates * (1.0 / GATE_SOFTCAP))
    ig_ref[...] = gates[:MLSTM_HEADS]
    z = gates[MLSTM_HEADS:]
    lf_ref[...] = -(jnp.maximum(-z, 0.0) + jnp.log1p(jnp.exp(-jnp.abs(z))))


def _mlstm_pre(h, g, wqk, wv, wog, wg, cw, cb, gb, tm, seq):
    m, d = h.shape
    hq = MLSTM_HEADS * MLSTM_DK
    hv = MLSTM_HEADS * MLSTM_DV
    row = lambda i: (i, 0)
    colblk = lambda i: (0, i)
    halo = lambda i: (jnp.maximum(i * (tm // CONV_HALO) - 1, 0), 0)
    return pl.pallas_call(
        functools.partial(_mlstm_pre_kernel, tiles_per_seq=seq // tm),
        out_shape=(jax.ShapeDtypeStruct((m, hq), BF16), jax.ShapeDtypeStruct((hq, m), BF16),
                   jax.ShapeDtypeStruct((m, hv), BF16), jax.ShapeDtypeStruct((m, hv), BF16),
                   jax.ShapeDtypeStruct((MLSTM_HEADS, m), F32), jax.ShapeDtypeStruct((MLSTM_HEADS, m), F32)),
        grid=(m // tm,),
        in_specs=[pl.BlockSpec((tm, d), row), pl.BlockSpec((CONV_HALO, d), halo), _const_spec(g.shape),
                  _const_spec(wqk.shape), _const_spec(wv.shape), _const_spec(wog.shape), _const_spec(wg.shape),
                  _const_spec(cw.shape), _const_spec(cb.shape), _const_spec(gb.shape)],
        out_specs=(pl.BlockSpec((tm, hq), row), pl.BlockSpec((hq, tm), colblk), pl.BlockSpec((tm, hv), row),
                   pl.BlockSpec((tm, hv), row), pl.BlockSpec((MLSTM_HEADS, tm), colblk),
                   pl.BlockSpec((MLSTM_HEADS, tm), colblk)),
        compiler_params=_params("parallel"),
        name="mlstm_pre",
    )(h, h, g, wqk, wv, wog, wg, cw, cb, gb)


def _split3(x):
    x1 = x.astype(BF16)
    r1 = x - x1.astype(F32)
    x2 = r1.astype(BF16)
    x3 = (r1 - x2.astype(F32)).astype(BF16)
    return x1, x2, x3


def _mlstm_chunk_kernel(q_ref, kt_ref, v_ref, og_ref, ig_ref, lf_ref, hg_ref, o_ref, ct_ref, m_ref):
    lc = q_ref.shape[0]
    pair = 2 * MLSTM_DK

    @pl.when(pl.program_id(1) == 0)
    def _():
        ct_ref[...] = jnp.zeros_like(ct_ref)
        m_ref[...] = jnp.zeros_like(m_ref)

    row = lax.broadcasted_iota(jnp.int32, (lc, lc), 0)
    col = lax.broadcasted_iota(jnp.int32, (lc, lc), 1)
    causal = row >= col
    upper = jnp.where(row <= col, 1.0, 0.0).astype(BF16)
    f = lf_ref[...]
    f1, f2, f3 = _split3(f)
    b = _dot(f1, upper) + _dot(f2, upper) + _dot(f3, upper)
    a = ig_ref[...] - b
    m_prev = m_ref[:, 0:1]
    m_last = jnp.maximum(m_prev, jnp.max(a, axis=1, keepdims=True))
    m_new = b[:, lc - 1:lc] + m_last
    a_st = jnp.exp(m_prev - m_last)
    w_st = jnp.exp(a - m_last)
    m_ref[...] = jnp.broadcast_to(m_new, m_ref.shape)

    lane = lax.broadcasted_iota(jnp.int32, (lc, pair), 1)
    ones_col = jnp.where(lane == 0, 1.0, 0.0).astype(BF16)
    srow = lax.broadcasted_iota(jnp.int32, (pair, 1), 0)
    hg = hg_ref[...]

    for p in range(MLSTM_HEADS // 2):
        qb = q_ref[:, p * pair:(p + 1) * pair]
        ktb = kt_ref[p * pair:(p + 1) * pair, :]
        ct = ct_ref[p]
        ct_b = ct.astype(BF16)
        upd = []
        for hh in range(2):
            h = 2 * p + hh
            qm = jnp.where((lane < MLSTM_DK) if hh == 0 else (lane >= MLSTM_DK), qb, jnp.zeros_like(qb))
            a_h = a[h:h + 1, :]
            d = jnp.where(causal, a_h, -jnp.inf)
            m_col = jnp.maximum(m_prev[h:h + 1, :], jnp.max(d, axis=1, keepdims=True))
            b_col = jnp.sum(jnp.where(causal, f[h:h + 1, :], 0.0), axis=1, keepdims=True)
            s = _dot(qm, ktb) * jnp.exp(d - m_col)
            vsl = slice(h * MLSTM_DV, (h + 1) * MLSTM_DV)
            vaug = jnp.concatenate([v_ref[:, vsl], ones_col], axis=1)
            nd = jnp.exp(m_prev[h:h + 1, :] - m_col) * _dot(qm, ct_b) + _dot(s.astype(BF16), vaug)
            den = jnp.maximum(jnp.abs(nd[:, MLSTM_DV:MLSTM_DV + 1]), jnp.exp(-(b_col + m_col)))
            hv = nd[:, :MLSTM_DV] / den
            hn = _rms(hv, hg[:, vsl])
            o_ref[:, vsl] = (og_ref[:, vsl].astype(F32) * hn).astype(o_ref.dtype)
            kw = (ktb.astype(F32) * w_st[h:h + 1, :]).astype(BF16)
            upd.append(_dot(kw, vaug))
        top = srow < MLSTM_DK
        decay = jnp.where(top, a_st[2 * p:2 * p + 1, :], a_st[2 * p + 1:2 * p + 2, :])
        ct_ref[p] = decay * ct + jnp.where(top, upd[0], upd[1])


def _mlstm_chunk(q, kt, v, og, ig, lf, hg, batch, seq, lc):
    m = q.shape[0]
    hq = MLSTM_HEADS * MLSTM_DK
    hv = MLSTM_HEADS * MLSTM_DV
    nc = seq // lc
    row = lambda b, c: (b * nc + c, 0)
    colblk = lambda b, c: (0, b * nc + c)
    return pl.pallas_call(
        _mlstm_chunk_kernel,
        out_shape=jax.ShapeDtypeStruct((m, hv), BF16),
        grid=(batch, nc),
        in_specs=[pl.BlockSpec((lc, hq), row), pl.BlockSpec((hq, lc), colblk), pl.BlockSpec((lc, hv), row),
                  pl.BlockSpec((lc, hv), row), pl.BlockSpec((MLSTM_HEADS, lc), colblk),
                  pl.BlockSpec((MLSTM_HEADS, lc), colblk), _const_spec(hg.shape)],
        out_specs=pl.BlockSpec((lc, hv), row),
        scratch_shapes=[pltpu.VMEM((MLSTM_HEADS // 2, 2 * MLSTM_DK, 2 * MLSTM_DV), F32),
                        pltpu.VMEM((MLSTM_HEADS, LANES), F32)],
        compiler_params=_params("parallel", "arbitrary"),
        name="mlstm_chunk",
    )(q, kt, v, og, ig, lf, hg)


def _head_block(nope, x1, x2):
    z = jnp.zeros(nope.shape[:-1] + (HEAD_BLOCK - MLA_NOPE - MLA_ROPE,), nope.dtype)
    return jnp.concatenate([nope[..., :48], x1, nope[..., 48:], z, x2], axis=-1)


def _mla_weights(w_in, w_uq, w_ukv):
    d = w_in.shape[0]
    half = MLA_ROPE // 2
    kr = w_in[:, MLA_Q_RANK + MLA_KV_RANK:]
    kr_blk = _head_block(jnp.zeros((d, MLA_NOPE), w_in.dtype), kr[:, :half], kr[:, half:])
    win = jnp.concatenate([w_in[:, :MLA_Q_RANK + MLA_KV_RANK], kr_blk], axis=1).astype(BF16)
    uq = w_uq.reshape(MLA_Q_RANK, MLA_HEADS, MLA_QK)
    wuq = _head_block(uq[..., :MLA_NOPE], uq[..., MLA_NOPE:MLA_NOPE + half], uq[..., MLA_NOPE + half:])
    wuq = wuq.reshape(MLA_Q_RANK, MLA_HEADS * HEAD_BLOCK).astype(BF16)
    ukv = w_ukv.reshape(MLA_KV_RANK, MLA_HEADS, MLA_NOPE + MLA_V)
    zr = jnp.zeros((MLA_KV_RANK, MLA_HEADS, half), w_ukv.dtype)
    wuk = _head_block(ukv[..., :MLA_NOPE], zr, zr).reshape(MLA_KV_RANK, MLA_HEADS * HEAD_BLOCK).astype(BF16)
    wuv = ukv[..., MLA_NOPE:].reshape(MLA_KV_RANK, MLA_HEADS * MLA_V).astype(BF16)
    return win, wuq, wuk, wuv


def _rope_inv_pattern():
    inv = ROPE_THETA ** (-jnp.arange(0, MLA_ROPE, 2, dtype=F32) / MLA_ROPE)
    z48 = jnp.zeros((48,), F32)
    return jnp.concatenate([z48, -inv, z48, inv]).reshape(1, LANES)


def kernel(x, positions, norm_mix_g, norm_mlp_g, final_g, mla_w_in, mla_q_norm_g, mla_kv_norm_g, mla_w_uq,
           mla_w_ukv, mla_w_o, mlstm_w_in, mlstm_conv_w, mlstm_conv_b, mlstm_i_b, mlstm_f_b,
           mlstm_head_norm_g, mlstm_w_o, mlp_w1, mlp_w2):
    batch, seq, d = x.shape
    m = batch * seq
    depth = norm_mix_g.shape[0]
    tm = min(512, seq)
    tq = min(256, seq)
    lc = min(256, seq)
    assert seq % tm == 0 and seq % tq == 0 and seq % lc == 0 and tm % CONV_HALO == 0

    h = x.reshape(m, d)
    cosp, sinp = _rope_tables(positions.reshape(m, 1), _rope_inv_pattern(), min(2048, m))
    q_scale = (MLA_QK ** -0.5) * math.log2(math.e)
    row = lambda v: v.reshape(1, -1)
    hq = MLSTM_HEADS * MLSTM_DK
    hv = MLSTM_HEADS * MLSTM_DV

    for layer in range(depth):
        j = layer // 2
        g_mix = row(norm_mix_g[layer])
        if layer % 2 == 0:
            win, wuq, wuk, wuv = _mla_weights(mla_w_in[j], mla_w_uq[j], mla_w_ukv[j])
            q, kt, v = _mla_pre(h, g_mix, win, row(mla_q_norm_g[j]), row(mla_kv_norm_g[j]), wuq, wuk, wuv,
                               cosp, sinp, tm, q_scale)
            mixed = _attention(q, kt, v, batch, seq, tq)
            w_o = mla_w_o[j]
        else:
            w = mlstm_w_in[j]
            wqk = w[:, :2 * hq].astype(BF16)
            wv = w[:, 2 * hq:2 * hq + hv].astype(BF16)
            wog = w[:, 2 * hq + hv:2 * hq + 2 * hv].astype(BF16)
            wg = w[:, 2 * hq + 2 * hv:].T.astype(BF16)
            gb = jnp.concatenate([mlstm_i_b[j], mlstm_f_b[j]]).reshape(2 * MLSTM_HEADS, 1)
            q, kt, v, og, ig, lf = _mlstm_pre(h, g_mix, wqk, wv, wog, wg, mlstm_conv_w[j],
                                              row(mlstm_conv_b[j]), gb, tm, seq)
            mixed = _mlstm_chunk(q, kt, v, og, ig, lf, row(mlstm_head_norm_g[j]), batch, seq, lc)
            w_o = mlstm_w_o[j]
        h = _post(mixed, w_o.astype(BF16), h, row(norm_mlp_g[layer]), mlp_w1[layer].astype(BF16),
                  mlp_w2[layer].astype(BF16), row(final_g), tm, final=(layer == depth - 1))
    return h.reshape(batch, seq, d)
```

```python
import functools
import math

import jax
import jax.numpy as jnp
from jax import lax
from jax.experimental import pallas as pl
from jax.experimental.pallas import tpu as pltpu

EPS = 1e-6
ROPE_THETA = 10000.0
GATE_SOFTCAP = 15.0

MLA_HEADS = 16
MLA_NOPE = 64
MLA_ROPE = 32
MLA_V = 64
MLA_QK = MLA_NOPE + MLA_ROPE
MLA_Q_RANK = 512
MLA_KV_RANK = 256

MLSTM_HEADS = 8
MLSTM_DK = 64
MLSTM_DV = 128
MLSTM_CONV = 4

LANES = 128
HEAD_BLOCK = LANES
KV_BLOCK = LANES
CONV_HALO = 16
VMEM_LIMIT = 56 * 1024 * 1024
LOG2E = math.log2(math.e)

BF16 = jnp.bfloat16
F32 = jnp.float32


def _rms(x, g):
    return x * lax.rsqrt(jnp.mean(x * x, axis=-1, keepdims=True) + EPS) * g


def _dot(a, b):
    return jnp.dot(a, b, preferred_element_type=F32)


def _sigmoid(x):
    return 1.0 / (1.0 + jnp.exp2(x * (-LOG2E)))


def _params(*sem):
    return pltpu.CompilerParams(dimension_semantics=sem, vmem_limit_bytes=VMEM_LIMIT)


def _const_spec(shape):
    return pl.BlockSpec(shape, lambda *_: (0,) * len(shape), pipeline_mode=pl.Buffered(1))


def _rope_kernel(pos_ref, inv_ref, cos_ref, sin_ref):
    ang = pos_ref[...].astype(F32) * inv_ref[...]
    cos_ref[...] = jnp.cos(ang)
    sin_ref[...] = jnp.sin(ang)


def _rope_tables(pos_col, inv_pat, tr):
    m = pos_col.shape[0]
    return pl.pallas_call(
        _rope_kernel,
        out_shape=(jax.ShapeDtypeStruct((m, LANES), F32),) * 2,
        grid=(m // tr,),
        in_specs=[pl.BlockSpec((tr, 1), lambda i: (i, 0)), _const_spec((1, LANES))],
        out_specs=(pl.BlockSpec((tr, LANES), lambda i: (i, 0)),) * 2,
        compiler_params=_params("parallel"),
        name="rope_tables",
    )(pos_col, inv_pat)


def _mla_pre_kernel(h_ref, g_ref, win_ref, gq_ref, gkv_ref, wuq_ref, wukt_ref, wuv_ref,
                    cos_ref, sin_ref, q_ref, kt_ref, v_ref, *, q_scale):
    u = _rms(h_ref[...], g_ref[...]).astype(BF16)
    proj = _dot(u, win_ref[...])
    cq = _rms(proj[:, :MLA_Q_RANK], gq_ref[...]).astype(BF16)
    ckv = _rms(proj[:, MLA_Q_RANK:MLA_Q_RANK + MLA_KV_RANK], gkv_ref[...]).astype(BF16)
    kr = proj[:, MLA_Q_RANK + MLA_KV_RANK:]
    c = cos_ref[...]
    s = sin_ref[...]
    half = HEAD_BLOCK // 2
    krt = (kr * c + pltpu.roll(kr, half, axis=1) * s).T
    q = _dot(cq, wuq_ref[...])
    knt = lax.dot_general(wukt_ref[...], ckv, (((1,), (1,)), ((), ())), preferred_element_type=F32)
    cs = c * q_scale
    ss = s * q_scale
    zpad = jnp.zeros((HEAD_BLOCK - MLA_NOPE - MLA_ROPE, krt.shape[1]), F32)
    for h in range(MLA_HEADS):
        sl = slice(h * HEAD_BLOCK, (h + 1) * HEAD_BLOCK)
        qh = q[:, sl]
        q_ref[:, sl] = (qh * cs + pltpu.roll(qh, half, axis=1) * ss).astype(BF16)
        nh = knt[h * MLA_NOPE:(h + 1) * MLA_NOPE, :]
        kt_ref[sl, :] = jnp.concatenate([nh[:48], krt[48:64], nh[48:], zpad, krt[112:]], axis=0).astype(BF16)
    v_ref[...] = _dot(ckv, wuv_ref[...]).astype(BF16)


def _mla_pre(h, g, win, gq, gkv, wuq, wuk, wuv, cosp, sinp, tm, q_scale):
    m, d = h.shape
    hq = MLA_HEADS * HEAD_BLOCK
    hv = MLA_HEADS * MLA_V
    row = lambda i: (i, 0)
    return pl.pallas_call(
        functools.partial(_mla_pre_kernel, q_scale=q_scale),
        out_shape=(jax.ShapeDtypeStruct((m, hq), BF16), jax.ShapeDtypeStruct((hq, m), BF16),
                   jax.ShapeDtypeStruct((m, hv), BF16)),
        grid=(m // tm,),
        in_specs=[pl.BlockSpec((tm, d), row), _const_spec(g.shape), _const_spec(win.shape),
                  _const_spec(gq.shape), _const_spec(gkv.shape), _const_spec(wuq.shape),
                  _const_spec(wuk.shape), _const_spec(wuv.shape),
                  pl.BlockSpec((tm, LANES), row), pl.BlockSpec((tm, LANES), row)],
        out_specs=(pl.BlockSpec((tm, hq), row), pl.BlockSpec((hq, tm), lambda i: (0, i)),
                   pl.BlockSpec((tm, hv), row)),
        compiler_params=_params("parallel"),
        name="mla_pre",
    )(h, g, win, gq, gkv, wuq, wuk, wuv, cosp, sinp)


def _attn_kernel(q_ref, kt_ref, v_ref, o_ref, kbd_ref, vbd_ref, *, tq):
    seq = q_ref.shape[0]
    nkb = seq // KV_BLOCK
    pw = 2 * KV_BLOCK

    @pl.when((pl.program_id(0) == 0) & (pl.program_id(1) == 0))
    def _():
        kbd_ref[...] = jnp.zeros_like(kbd_ref)

    lane_v = lax.broadcasted_iota(jnp.int32, (KV_BLOCK, 2 * MLA_V), 1)
    for j in range(nkb):
        ks = slice(j * KV_BLOCK, (j + 1) * KV_BLOCK)
        kbd_ref[:HEAD_BLOCK, j * pw:j * pw + KV_BLOCK] = kt_ref[:HEAD_BLOCK, ks]
        kbd_ref[HEAD_BLOCK:, j * pw + KV_BLOCK:(j + 1) * pw] = kt_ref[HEAD_BLOCK:, ks]
        vb = v_ref[ks, :]
        vbd_ref[j * pw:j * pw + KV_BLOCK, :] = jnp.where(lane_v < MLA_V, vb, jnp.zeros_like(vb))
        vbd_ref[j * pw + KV_BLOCK:(j + 1) * pw, :] = jnp.where(lane_v >= MLA_V, vb, jnp.zeros_like(vb))

    rel = (lax.broadcasted_iota(jnp.int32, (tq, KV_BLOCK), 0)
           - lax.broadcasted_iota(jnp.int32, (tq, KV_BLOCK), 1))
    lane_o = lax.broadcasted_iota(jnp.int32, (tq, 2 * MLA_V), 1)

    def scores(qi):
        nb = (qi + 1) * tq // KV_BLOCK
        return _dot(q_ref[qi * tq:(qi + 1) * tq, :], kbd_ref[:, :nb * pw])

    nq = seq // tq
    ahead = 1
    pending = [scores(t) for t in range(min(ahead, nq))]
    for qi in range(nq):
        q0 = qi * tq
        nb = (q0 + tq) // KV_BLOCK
        s = pending.pop(0)
        if qi + ahead < nq:
            pending.append(scores(qi + ahead))
        chunks = []
        for j in range(nb):
            for hh in range(2):
                c = s[:, (2 * j + hh) * KV_BLOCK:(2 * j + hh + 1) * KV_BLOCK]
                if (j + 1) * KV_BLOCK - 1 > q0:
                    c = jnp.where(rel >= j * KV_BLOCK - q0, c, -jnp.inf)
                chunks.append(c)
        m = [jnp.max(functools.reduce(jnp.maximum, chunks[hh::2]), axis=-1, keepdims=True) for hh in range(2)]
        lsum = [None, None]
        p = []
        for idx, c in enumerate(chunks):
            e = jnp.exp2(c - m[idx % 2])
            lsum[idx % 2] = e if lsum[idx % 2] is None else lsum[idx % 2] + e
            p.append(e.astype(BF16))
        o = _dot(jnp.concatenate(p, axis=1), vbd_ref[:nb * pw, :])
        inv = [1.0 / jnp.sum(l, axis=-1, keepdims=True) for l in lsum]
        o_ref[q0:q0 + tq, :] = (o * jnp.where(lane_o < MLA_V, inv[0], inv[1])).astype(o_ref.dtype)


def _attention(q, kt, v, batch, seq, tq):
    m = q.shape[0]
    pairs = MLA_HEADS // 2
    nkb = seq // KV_BLOCK
    blk = lambda b, p: (b, p)
    return pl.pallas_call(
        functools.partial(_attn_kernel, tq=tq),
        out_shape=jax.ShapeDtypeStruct((m, MLA_HEADS * MLA_V), BF16),
        grid=(batch, pairs),
        in_specs=[pl.BlockSpec((seq, 2 * HEAD_BLOCK), blk),
                  pl.BlockSpec((2 * HEAD_BLOCK, seq), lambda b, p: (p, b)),
                  pl.BlockSpec((seq, 2 * MLA_V), blk)],
        out_specs=pl.BlockSpec((seq, 2 * MLA_V), blk),
        scratch_shapes=[pltpu.VMEM((2 * HEAD_BLOCK, nkb * 2 * KV_BLOCK), BF16),
                        pltpu.VMEM((nkb * 2 * KV_BLOCK, 2 * MLA_V), BF16)],
        compiler_params=_params("arbitrary", "arbitrary"),
        name="mla_attention",
    )(q, kt, v)


def _post_kernel(a_ref, wo_ref, h_ref, g_ref, w1_ref, w2_ref, gf_ref, o_ref, *, hidden_chunk, final):
    h1 = h_ref[...] + _dot(a_ref[...], wo_ref[...])
    u = _rms(h1, g_ref[...]).astype(BF16)
    acc = h1
    hidden = w1_ref.shape[1]
    for c in range(hidden // hidden_chunk):
        sl = slice(c * hidden_chunk, (c + 1) * hidden_chunk)
        a = jnp.maximum(_dot(u, w1_ref[:, sl]), 0.0)
        acc = acc + _dot((a * a).astype(BF16), w2_ref[sl, :])
    if final:
        acc = _rms(acc, gf_ref[...])
    o_ref[...] = acc


def _post(a, wo, h, g, w1, w2, gf, tm, final):
    m, d = h.shape
    row = lambda i: (i, 0)
    return pl.pallas_call(
        functools.partial(_post_kernel, hidden_chunk=1024, final=final),
        out_shape=jax.ShapeDtypeStruct((m, d), F32),
        grid=(m // tm,),
        in_specs=[pl.BlockSpec((tm, a.shape[1]), row), _const_spec(wo.shape), pl.BlockSpec((tm, d), row),
                  _const_spec(g.shape), _const_spec(w1.shape), _const_spec(w2.shape), _const_spec(gf.shape)],
        out_specs=pl.BlockSpec((tm, d), row),
        compiler_params=_params("parallel"),
        name="post_mlp",
    )(a, wo, h, g, w1, w2, gf)


def _mlstm_pre_kernel(h_ref, halo_ref, g_ref, wqk_ref, wv_ref, wog_ref, wg_ref, cw_ref, cb_ref, gb_ref,
                      q_ref, kt_ref, v_ref, og_ref, ig_ref, lf_ref, conv_ref, *, tiles_per_seq):
    tm = h_ref.shape[0]
    g = g_ref[...]
    u = _rms(h_ref[...], g).astype(BF16)
    first = pl.program_id(0) % tiles_per_seq == 0
    uh = jnp.where(first, 0.0, _rms(halo_ref[...], g)).astype(BF16)
    qk = _dot(jnp.concatenate([uh, u], axis=0), wqk_ref[...])
    nslab = qk.shape[1] // LANES
    for c in range(nslab):
        conv_ref[c] = qk[:, c * LANES:(c + 1) * LANES]
    v_ref[...] = _dot(u, wv_ref[...]).astype(BF16)
    o_pre = _dot(u, wog_ref[...])
    gates = lax.dot_general(wg_ref[...], u, (((1,), (1,)), ((), ())), preferred_element_type=F32) + gb_ref[...]
    cw = cw_ref[...]
    cb = cb_ref[...]
    hq = MLSTM_HEADS * MLSTM_DK
    for c in range(nslab):
        sl = slice(c * LANES, (c + 1) * LANES)
        y = cb[:, sl]
        for j in range(MLSTM_CONV):
            off = CONV_HALO - (MLSTM_CONV - 1) + j
            y = y + conv_ref[c, off:off + tm, :] * cw[j:j + 1, sl]
        y = y * _sigmoid(y)
        if c * LANES < hq:
            q_ref[:, sl] = y.astype(BF16)
        else:
            kt_ref[c * LANES - hq:(c + 1) * LANES - hq, :] = (y * (MLSTM_DK ** -0.5)).T.astype(BF16)
    og_ref[...] = _sigmoid(o_pre).astype(BF16)
    gates = GATE_SOFTCAP * jnp.tanh(gates * (1.0 / GATE_SOFTCAP))
    ig_ref[...] = gates[:MLSTM_HEADS]
    z = gates[MLSTM_HEADS:]
    lf_ref[...] = -(jnp.maximum(-z, 0.0) + jnp.log1p(jnp.exp(-jnp.abs(z))))


def _mlstm_pre(h, g, wqk, wv, wog, wg, cw, cb, gb, tm, seq):
    m, d = h.shape
    hq = MLSTM_HEADS * MLSTM_DK
    hv = MLSTM_HEADS * MLSTM_DV
    row = lambda i: (i, 0)
    colblk = lambda i: (0, i)
    halo = lambda i: (jnp.maximum(i * (tm // CONV_HALO) - 1, 0), 0)
    return pl.pallas_call(
        functools.partial(_mlstm_pre_kernel, tiles_per_seq=seq // tm),
        out_shape=(jax.ShapeDtypeStruct((m, hq), BF16), jax.ShapeDtypeStruct((hq, m), BF16),
                   jax.ShapeDtypeStruct((m, hv), BF16), jax.ShapeDtypeStruct((m, hv), BF16),
                   jax.ShapeDtypeStruct((MLSTM_HEADS, m), F32), jax.ShapeDtypeStruct((MLSTM_HEADS, m), F32)),
        grid=(m // tm,),
        in_specs=[pl.BlockSpec((tm, d), row), pl.BlockSpec((CONV_HALO, d), halo), _const_spec(g.shape),
                  _const_spec(wqk.shape), _const_spec(wv.shape), _const_spec(wog.shape), _const_spec(wg.shape),
                  _const_spec(cw.shape), _const_spec(cb.shape), _const_spec(gb.shape)],
        out_specs=(pl.BlockSpec((tm, hq), row), pl.BlockSpec((hq, tm), colblk), pl.BlockSpec((tm, hv), row),
                   pl.BlockSpec((tm, hv), row), pl.BlockSpec((MLSTM_HEADS, tm), colblk),
                   pl.BlockSpec((MLSTM_HEADS, tm), colblk)),
        scratch_shapes=[pltpu.VMEM((2 * hq // LANES, CONV_HALO + tm, LANES), F32)],
        compiler_params=_params("parallel"),
        name="mlstm_pre",
    )(h, h, g, wqk, wv, wog, wg, cw, cb, gb)


def _split3(x):
    x1 = x.astype(BF16)
    r1 = x - x1.astype(F32)
    x2 = r1.astype(BF16)
    x3 = (r1 - x2.astype(F32)).astype(BF16)
    return x1, x2, x3


def _mlstm_chunk_kernel(q_ref, kt_ref, v_ref, og_ref, ig_ref, lf_ref, hg_ref, o_ref, ct_ref, m_ref):
    lc = q_ref.shape[0]
    pair = 2 * MLSTM_DK

    @pl.when(pl.program_id(1) == 0)
    def _():
        ct_ref[...] = jnp.zeros_like(ct_ref)
        m_ref[...] = jnp.zeros_like(m_ref)

    row = lax.broadcasted_iota(jnp.int32, (lc, lc), 0)
    col = lax.broadcasted_iota(jnp.int32, (lc, lc), 1)
    causal = row >= col
    upper = jnp.where(row <= col, 1.0, 0.0).astype(BF16)
    f = lf_ref[...]
    f1, f2, f3 = _split3(f)
    b = _dot(f1, upper) + _dot(f2, upper) + _dot(f3, upper)
    a = ig_ref[...] - b
    m_prev = m_ref[...]
    a_max = jnp.max(a, axis=1, keepdims=True)
    m_last = jnp.maximum(m_prev, a_max)
    m_ref[...] = b[:, lc - 1:lc] + m_last
    a_st = jnp.exp(m_prev - m_last)
    w_st = jnp.exp(a - jnp.maximum(m_prev[:, 0:1], a_max))
    a2 = a * LOG2E
    f2 = f * LOG2E
    m2_prev = m_prev * LOG2E

    lane = lax.broadcasted_iota(jnp.int32, (lc, pair), 1)
    ones = jnp.ones((lc, MLSTM_DV), BF16)
    top = lax.broadcasted_iota(jnp.int32, (pair, MLSTM_DV), 0) < MLSTM_DK
    hg = hg_ref[...]

    for p in range(MLSTM_HEADS // 2):
        qb = q_ref[:, p * pair:(p + 1) * pair]
        ktb = kt_ref[p * pair:(p + 1) * pair, :]
        ct = ct_ref[p]
        ct_b = ct.astype(BF16)
        upd = []
        for hh in range(2):
            h = 2 * p + hh
            qm = jnp.where((lane < MLSTM_DK) if hh == 0 else (lane >= MLSTM_DK), qb, jnp.zeros_like(qb))
            d = jnp.where(causal, a2[h:h + 1, :], -jnp.inf)
            m_h = jnp.broadcast_to(m2_prev[h:h + 1, :], (lc, MLSTM_DV))
            m_col = jnp.maximum(m_h, jnp.max(d, axis=1, keepdims=True))
            b_col = jnp.sum(jnp.where(causal, f2[h:h + 1, :], 0.0), axis=1, keepdims=True)
            s = _dot(qm, ktb) * jnp.exp2(d - jnp.concatenate([m_col] * (lc // MLSTM_DV), axis=1))
            vsl = slice(h * MLSTM_DV, (h + 1) * MLSTM_DV)
            vaug = jnp.concatenate([v_ref[:, vsl], ones], axis=1)
            q_inter = (qm.astype(F32) * jnp.exp2(m_h - m_col)).astype(BF16)
            nd = _dot(jnp.concatenate([s.astype(BF16), q_inter], axis=1),
                      jnp.concatenate([vaug, ct_b], axis=0))
            den = jnp.maximum(jnp.abs(nd[:, MLSTM_DV:]), jnp.exp2(-(b_col + m_col)))
            hn = _rms(nd[:, :MLSTM_DV] / den, hg[:, vsl])
            o_ref[:, vsl] = (og_ref[:, vsl].astype(F32) * hn).astype(o_ref.dtype)
            kw = (ktb.astype(F32) * w_st[h:h + 1, :]).astype(BF16)
            upd.append(_dot(kw, vaug))
        decay = jnp.where(top, a_st[2 * p:2 * p + 1, :], a_st[2 * p + 1:2 * p + 2, :])
        top2 = jnp.concatenate([top, top], axis=1)
        ct_ref[p] = jnp.concatenate([decay, decay], axis=1) * ct + jnp.where(top2, upd[0], upd[1])


def _mlstm_chunk(q, kt, v, og, ig, lf, hg, batch, seq, lc):
    m = q.shape[0]
    hq = MLSTM_HEADS * MLSTM_DK
    hv = MLSTM_HEADS * MLSTM_DV
    nc = seq // lc
    row = lambda b, c: (b * nc + c, 0)
    colblk = lambda b, c: (0, b * nc + c)
    return pl.pallas_call(
        _mlstm_chunk_kernel,
        out_shape=jax.ShapeDtypeStruct((m, hv), BF16),
        grid=(batch, nc),
        in_specs=[pl.BlockSpec((lc, hq), row), pl.BlockSpec((hq, lc), colblk), pl.BlockSpec((lc, hv), row),
                  pl.BlockSpec((lc, hv), row), pl.BlockSpec((MLSTM_HEADS, lc), colblk),
                  pl.BlockSpec((MLSTM_HEADS, lc), colblk), _const_spec(hg.shape)],
        out_specs=pl.BlockSpec((lc, hv), row),
        scratch_shapes=[pltpu.VMEM((MLSTM_HEADS // 2, 2 * MLSTM_DK, 2 * MLSTM_DV), F32),
                        pltpu.VMEM((MLSTM_HEADS, LANES), F32)],
        compiler_params=_params("parallel", "arbitrary"),
        name="mlstm_chunk",
    )(q, kt, v, og, ig, lf, hg)


def _head_block(nope, x1, x2):
    z = jnp.zeros(nope.shape[:-1] + (HEAD_BLOCK - MLA_NOPE - MLA_ROPE,), nope.dtype)
    return jnp.concatenate([nope[..., :48], x1, nope[..., 48:], z, x2], axis=-1)


def _mla_weights(w_in, w_uq, w_ukv):
    d = w_in.shape[0]
    half = MLA_ROPE // 2
    kr = w_in[:, MLA_Q_RANK + MLA_KV_RANK:]
    kr_blk = _head_block(jnp.zeros((d, MLA_NOPE), w_in.dtype), kr[:, :half], kr[:, half:])
    win = jnp.concatenate([w_in[:, :MLA_Q_RANK + MLA_KV_RANK], kr_blk], axis=1).astype(BF16)
    uq = w_uq.reshape(MLA_Q_RANK, MLA_HEADS, MLA_QK)
    wuq = _head_block(uq[..., :MLA_NOPE], uq[..., MLA_NOPE:MLA_NOPE + half], uq[..., MLA_NOPE + half:])
    wuq = wuq.reshape(MLA_Q_RANK, MLA_HEADS * HEAD_BLOCK).astype(BF16)
    ukv = w_ukv.reshape(MLA_KV_RANK, MLA_HEADS, MLA_NOPE + MLA_V)
    wuk = ukv[..., :MLA_NOPE].reshape(MLA_KV_RANK, MLA_HEADS * MLA_NOPE).T.astype(BF16)
    wuv = ukv[..., MLA_NOPE:].reshape(MLA_KV_RANK, MLA_HEADS * MLA_V).astype(BF16)
    return win, wuq, wuk, wuv


def _rope_inv_pattern():
    inv = ROPE_THETA ** (-jnp.arange(0, MLA_ROPE, 2, dtype=F32) / MLA_ROPE)
    z48 = jnp.zeros((48,), F32)
    return jnp.concatenate([z48, -inv, z48, inv]).reshape(1, LANES)


def kernel(x, positions, norm_mix_g, norm_mlp_g, final_g, mla_w_in, mla_q_norm_g, mla_kv_norm_g, mla_w_uq,
           mla_w_ukv, mla_w_o, mlstm_w_in, mlstm_conv_w, mlstm_conv_b, mlstm_i_b, mlstm_f_b,
           mlstm_head_norm_g, mlstm_w_o, mlp_w1, mlp_w2):
    batch, seq, d = x.shape
    m = batch * seq
    depth = norm_mix_g.shape[0]
    tm = min(512, seq)
    tq = min(256, seq)
    lc = min(256, seq)
    assert seq % tm == 0 and seq % tq == 0 and seq % lc == 0 and tm % CONV_HALO == 0

    h = x.reshape(m, d)
    cosp, sinp = _rope_tables(positions.reshape(m, 1), _rope_inv_pattern(), min(2048, m))
    q_scale = (MLA_QK ** -0.5) * math.log2(math.e)
    row = lambda v: v.reshape(1, -1)
    hq = MLSTM_HEADS * MLSTM_DK
    hv = MLSTM_HEADS * MLSTM_DV

    for layer in range(depth):
        j = layer // 2
        g_mix = row(norm_mix_g[layer])
        if layer % 2 == 0:
            win, wuq, wuk, wuv = _mla_weights(mla_w_in[j], mla_w_uq[j], mla_w_ukv[j])
            q, kt, v = _mla_pre(h, g_mix, win, row(mla_q_norm_g[j]), row(mla_kv_norm_g[j]), wuq, wuk, wuv,
                               cosp, sinp, tm, q_scale)
            mixed = _attention(q, kt, v, batch, seq, tq)
            w_o = mla_w_o[j]
        else:
            w = mlstm_w_in[j]
            wqk = w[:, :2 * hq].astype(BF16)
            wv = w[:, 2 * hq:2 * hq + hv].astype(BF16)
            wog = w[:, 2 * hq + hv:2 * hq + 2 * hv].astype(BF16)
            wg = w[:, 2 * hq + 2 * hv:].T.astype(BF16)
            gb = jnp.concatenate([mlstm_i_b[j], mlstm_f_b[j]]).reshape(2 * MLSTM_HEADS, 1)
            q, kt, v, og, ig, lf = _mlstm_pre(h, g_mix, wqk, wv, wog, wg, mlstm_conv_w[j],
                                              row(mlstm_conv_b[j]), gb, tm, seq)
            mixed = _mlstm_chunk(q, kt, v, og, ig, lf, row(mlstm_head_norm_g[j]), batch, seq, lc)
            w_o = mlstm_w_o[j]
        h = _post(mixed, w_o.astype(BF16), h, row(norm_mlp_g[layer]), mlp_w1[layer].astype(BF16),
                  mlp_w2[layer].astype(BF16), row(final_g), tm, final=(layer == depth - 1))
    return h.reshape(batch, seq, d)
```

```python
import functools
import math

import jax
import jax.numpy as jnp
from jax import lax
from jax.experimental import pallas as pl
from jax.experimental.pallas import tpu as pltpu

EPS = 1e-6
ROPE_THETA = 10000.0
GATE_SOFTCAP = 15.0

MLA_HEADS = 16
MLA_NOPE = 64
MLA_ROPE = 32
MLA_V = 64
MLA_QK = MLA_NOPE + MLA_ROPE
MLA_Q_RANK = 512
MLA_KV_RANK = 256

MLSTM_HEADS = 8
MLSTM_DK = 64
MLSTM_DV = 128
MLSTM_CONV = 4

LANES = 128
HEAD_BLOCK = LANES
KV_BLOCK = LANES
ATTN_ROWS = 32
CONV_HALO = 16
VMEM_LIMIT = 56 * 1024 * 1024
LOG2E = math.log2(math.e)

BF16 = jnp.bfloat16
F32 = jnp.float32


def _rms(x, g):
    return x * lax.rsqrt(jnp.mean(x * x, axis=-1, keepdims=True) + EPS) * g


def _dot(a, b):
    return jnp.dot(a, b, preferred_element_type=F32)


def _sigmoid(x):
    return 1.0 / (1.0 + jnp.exp2(x * (-LOG2E)))


def _params(*sem):
    return pltpu.CompilerParams(dimension_semantics=sem, vmem_limit_bytes=VMEM_LIMIT)


def _const_spec(shape):
    return pl.BlockSpec(shape, lambda *_: (0,) * len(shape), pipeline_mode=pl.Buffered(1))


def _rope_kernel(pos_ref, inv_ref, cos_ref, sin_ref):
    ang = pos_ref[...].astype(F32) * inv_ref[...]
    cos_ref[...] = jnp.cos(ang)
    sin_ref[...] = jnp.sin(ang)


def _rope_tables(pos_col, inv_pat, tr):
    m = pos_col.shape[0]
    return pl.pallas_call(
        _rope_kernel,
        out_shape=(jax.ShapeDtypeStruct((m, LANES), F32),) * 2,
        grid=(m // tr,),
        in_specs=[pl.BlockSpec((tr, 1), lambda i: (i, 0)), _const_spec((1, LANES))],
        out_specs=(pl.BlockSpec((tr, LANES), lambda i: (i, 0)),) * 2,
        compiler_params=_params("parallel"),
        name="rope_tables",
    )(pos_col, inv_pat)


def _mla_pre_kernel(h_ref, g_ref, win_ref, gq_ref, gkv_ref, wuq_ref, wukt_ref, wuv_ref,
                    cos_ref, sin_ref, q_ref, kt_ref, v_ref, *, q_scale):
    u = _rms(h_ref[...], g_ref[...]).astype(BF16)
    proj = _dot(u, win_ref[...])
    cq = _rms(proj[:, :MLA_Q_RANK], gq_ref[...]).astype(BF16)
    ckv = _rms(proj[:, MLA_Q_RANK:MLA_Q_RANK + MLA_KV_RANK], gkv_ref[...]).astype(BF16)
    kr = proj[:, MLA_Q_RANK + MLA_KV_RANK:]
    c = cos_ref[...]
    s = sin_ref[...]
    half = HEAD_BLOCK // 2
    krt = (kr * c + pltpu.roll(kr, half, axis=1) * s).T
    q = _dot(cq, wuq_ref[...])
    knt = lax.dot_general(wukt_ref[...], ckv, (((1,), (1,)), ((), ())), preferred_element_type=F32)
    cs = c * q_scale
    ss = s * q_scale
    zpad = jnp.zeros((HEAD_BLOCK - MLA_NOPE - MLA_ROPE, krt.shape[1]), F32)
    for h in range(MLA_HEADS):
        sl = slice(h * HEAD_BLOCK, (h + 1) * HEAD_BLOCK)
        qh = q[:, sl]
        q_ref[:, sl] = (qh * cs + pltpu.roll(qh, half, axis=1) * ss).astype(BF16)
        nh = knt[h * MLA_NOPE:(h + 1) * MLA_NOPE, :]
        kt_ref[sl, :] = jnp.concatenate([nh[:48], krt[48:64], nh[48:], zpad, krt[112:]], axis=0).astype(BF16)
    v_ref[...] = _dot(ckv, wuv_ref[...]).astype(BF16)


def _mla_pre(h, g, win, gq, gkv, wuq, wuk, wuv, cosp, sinp, tm, q_scale):
    m, d = h.shape
    hq = MLA_HEADS * HEAD_BLOCK
    hv = MLA_HEADS * MLA_V
    row = lambda i: (i, 0)
    return pl.pallas_call(
        functools.partial(_mla_pre_kernel, q_scale=q_scale),
        out_shape=(jax.ShapeDtypeStruct((m, hq), BF16), jax.ShapeDtypeStruct((hq, m), BF16),
                   jax.ShapeDtypeStruct((m, hv), BF16)),
        grid=(m // tm,),
        in_specs=[pl.BlockSpec((tm, d), row), _const_spec(g.shape), _const_spec(win.shape),
                  _const_spec(gq.shape), _const_spec(gkv.shape), _const_spec(wuq.shape),
                  _const_spec(wuk.shape), _const_spec(wuv.shape),
                  pl.BlockSpec((tm, LANES), row), pl.BlockSpec((tm, LANES), row)],
        out_specs=(pl.BlockSpec((tm, hq), row), pl.BlockSpec((hq, tm), lambda i: (0, i)),
                   pl.BlockSpec((tm, hv), row)),
        compiler_params=_params("parallel"),
        name="mla_pre",
    )(h, g, win, gq, gkv, wuq, wuk, wuv, cosp, sinp)


def _attn_kernel(q_ref, kt_ref, v_ref, o_ref, kbd_ref, vbd_ref, s_ref, p_ref, *, tq):
    seq = q_ref.shape[0]
    nkb = seq // KV_BLOCK
    pw = 2 * KV_BLOCK

    vw = 2 * MLA_V
    lane_v = lax.broadcasted_iota(jnp.int32, (KV_BLOCK, vw), 1)
    first_head = lane_v < MLA_V

    @pl.when((pl.program_id(0) == 0) & (pl.program_id(1) == 0))
    def _():
        kbd_ref[...] = jnp.zeros_like(kbd_ref)
        for j in range(nkb):
            vbd_ref[j * pw:j * pw + KV_BLOCK, vw:] = jnp.where(first_head, 1.0, 0.0).astype(BF16)
            vbd_ref[j * pw + KV_BLOCK:(j + 1) * pw, vw:] = jnp.where(first_head, 0.0, 1.0).astype(BF16)

    for j in range(nkb):
        ks = slice(j * KV_BLOCK, (j + 1) * KV_BLOCK)
        kbd_ref[:HEAD_BLOCK, j * pw:j * pw + KV_BLOCK] = kt_ref[:HEAD_BLOCK, ks]
        kbd_ref[HEAD_BLOCK:, j * pw + KV_BLOCK:(j + 1) * pw] = kt_ref[HEAD_BLOCK:, ks]
        vb = v_ref[ks, :]
        vbd_ref[j * pw:j * pw + KV_BLOCK, :vw] = jnp.where(first_head, vb, jnp.zeros_like(vb))
        vbd_ref[j * pw + KV_BLOCK:(j + 1) * pw, :vw] = jnp.where(first_head, jnp.zeros_like(vb), vb)

    row = lax.broadcasted_iota(jnp.int32, (tq, pw), 0)
    col = lax.broadcasted_iota(jnp.int32, (tq, pw), 1)
    rel = row - jnp.where(col < KV_BLOCK, col, col - KV_BLOCK)

    def scores(qi):
        q0 = qi * tq
        qv = q_ref[q0:q0 + tq, :]
        for j in range((q0 + tq) // KV_BLOCK):
            blk = _dot(qv, kbd_ref[:, j * pw:(j + 1) * pw])
            if (j + 1) * KV_BLOCK - 1 > q0:
                blk = jnp.where(rel >= j * KV_BLOCK - q0, blk, -jnp.inf)
            s_ref[qi % 2, :, j * pw:(j + 1) * pw] = blk

    def head_rows(x, fn):
        return jnp.concatenate([jnp.broadcast_to(fn(x[:, hh * KV_BLOCK:(hh + 1) * KV_BLOCK], axis=-1, keepdims=True),
                                                 (x.shape[0], KV_BLOCK)) for hh in range(2)], axis=1)

    nq = seq // tq
    scores(0)
    for qi in range(nq):
        q0 = qi * tq
        nb = (q0 + tq) // KV_BLOCK
        slot = qi % 2
        if qi + 1 < nq:
            scores(qi + 1)
        for r in range(tq // ATTN_ROWS):
            rows = slice(r * ATTN_ROWS, (r + 1) * ATTN_ROWS)
            nbr = (q0 + (r + 1) * ATTN_ROWS - 1) // KV_BLOCK + 1
            mx = functools.reduce(jnp.maximum, [s_ref[slot, rows, j * pw:(j + 1) * pw] for j in range(nbr)])
            m = head_rows(mx, jnp.max)
            for j in range(nb):
                if j < nbr:
                    p_ref[slot, rows, j * pw:(j + 1) * pw] = jnp.exp2(s_ref[slot, rows, j * pw:(j + 1) * pw] - m).astype(BF16)
                else:
                    p_ref[slot, rows, j * pw:(j + 1) * pw] = jnp.zeros((ATTN_ROWS, pw), BF16)
        o = _dot(p_ref[slot, :, :nb * pw], vbd_ref[:nb * pw, :])
        o_ref[q0:q0 + tq, :] = (o[:, :vw] / o[:, vw:]).astype(o_ref.dtype)


def _attention(q, kt, v, batch, seq, tq):
    m = q.shape[0]
    pairs = MLA_HEADS // 2
    nkb = seq // KV_BLOCK
    blk = lambda b, p: (b, p)
    return pl.pallas_call(
        functools.partial(_attn_kernel, tq=tq),
        out_shape=jax.ShapeDtypeStruct((m, MLA_HEADS * MLA_V), BF16),
        grid=(batch, pairs),
        in_specs=[pl.BlockSpec((seq, 2 * HEAD_BLOCK), blk),
                  pl.BlockSpec((2 * HEAD_BLOCK, seq), lambda b, p: (p, b)),
                  pl.BlockSpec((seq, 2 * MLA_V), blk)],
        out_specs=pl.BlockSpec((seq, 2 * MLA_V), blk),
        scratch_shapes=[pltpu.VMEM((2 * HEAD_BLOCK, nkb * 2 * KV_BLOCK), BF16),
                        pltpu.VMEM((nkb * 2 * KV_BLOCK, 4 * MLA_V), BF16),
                        pltpu.VMEM((2, tq, nkb * 2 * KV_BLOCK), F32),
                        pltpu.VMEM((2, tq, nkb * 2 * KV_BLOCK), BF16)],
        compiler_params=_params("arbitrary", "arbitrary"),
        name="mla_attention",
    )(q, kt, v)


def _post_kernel(a_ref, wo_ref, h_ref, g_ref, w1_ref, w2_ref, gf_ref, o_ref, *, hidden_chunk, final):
    h1 = h_ref[...] + _dot(a_ref[...], wo_ref[...])
    u = _rms(h1, g_ref[...]).astype(BF16)
    acc = h1
    hidden = w1_ref.shape[1]
    for c in range(hidden // hidden_chunk):
        sl = slice(c * hidden_chunk, (c + 1) * hidden_chunk)
        a = jnp.maximum(_dot(u, w1_ref[:, sl]), 0.0)
        acc = acc + _dot((a * a).astype(BF16), w2_ref[sl, :])
    if final:
        acc = _rms(acc, gf_ref[...])
    o_ref[...] = acc


def _post(a, wo, h, g, w1, w2, gf, tm, final):
    m, d = h.shape
    row = lambda i: (i, 0)
    return pl.pallas_call(
        functools.partial(_post_kernel, hidden_chunk=1024, final=final),
        out_shape=jax.ShapeDtypeStruct((m, d), F32),
        grid=(m // tm,),
        in_specs=[pl.BlockSpec((tm, a.shape[1]), row), _const_spec(wo.shape), pl.BlockSpec((tm, d), row),
                  _const_spec(g.shape), _const_spec(w1.shape), _const_spec(w2.shape), _const_spec(gf.shape)],
        out_specs=pl.BlockSpec((tm, d), row),
        compiler_params=_params("parallel"),
        name="post_mlp",
    )(a, wo, h, g, w1, w2, gf)


def _split3(x):
    x1 = x.astype(BF16)
    r1 = x - x1.astype(F32)
    x2 = r1.astype(BF16)
    x3 = (r1 - x2.astype(F32)).astype(BF16)
    return x1, x2, x3


def _mlstm_pre_kernel(h_ref, halo_ref, g_ref, wqk_ref, wv_ref, wog_ref, wg_ref, cw_ref, cb_ref, gb_ref, hg_ref,
                      q_ref, kt_ref, v_ref, og_ref, a2_ref, f2_ref, conv_ref, *, tiles_per_seq, lc):
    tm = h_ref.shape[0]
    g = g_ref[...]
    u = _rms(h_ref[...], g).astype(BF16)
    first = pl.program_id(0) % tiles_per_seq == 0
    uh = jnp.where(first, 0.0, _rms(halo_ref[...], g)).astype(BF16)
    qk = _dot(jnp.concatenate([uh, u], axis=0), wqk_ref[...])
    nslab = qk.shape[1] // LANES
    for c in range(nslab):
        conv_ref[c] = qk[:, c * LANES:(c + 1) * LANES]
    v_ref[...] = _dot(u, wv_ref[...]).astype(BF16)
    o_pre = _dot(u, wog_ref[...])
    gates = lax.dot_general(wg_ref[...], u, (((1,), (1,)), ((), ())), preferred_element_type=F32) + gb_ref[...]
    cw = cw_ref[...]
    cb = cb_ref[...]
    hq = MLSTM_HEADS * MLSTM_DK
    for c in range(nslab):
        sl = slice(c * LANES, (c + 1) * LANES)
        y = cb[:, sl]
        for j in range(MLSTM_CONV):
            off = CONV_HALO - (MLSTM_CONV - 1) + j
            y = y + conv_ref[c, off:off + tm, :] * cw[j:j + 1, sl]
        y = y * _sigmoid(y)
        if c * LANES < hq:
            q_ref[:, sl] = y.astype(BF16)
        else:
            kt_ref[c * LANES - hq:(c + 1) * LANES - hq, :] = (y * (MLSTM_DK ** -0.5)).T.astype(BF16)
    og_ref[...] = (_sigmoid(o_pre) * hg_ref[...]).astype(BF16)
    gates = GATE_SOFTCAP * jnp.tanh(gates * (1.0 / GATE_SOFTCAP))
    z = gates[MLSTM_HEADS:]
    lf = -(jnp.maximum(-z, 0.0) + jnp.log1p(jnp.exp(-jnp.abs(z))))
    upper = (lax.broadcasted_iota(jnp.int32, (lc, lc), 0) <= lax.broadcasted_iota(jnp.int32, (lc, lc), 1))
    upper = jnp.where(upper, 1.0, 0.0).astype(BF16)
    for c in range(tm // lc):
        seg = slice(c * lc, (c + 1) * lc)
        b = sum(_dot(piece, upper) for piece in _split3(lf[:, seg]))
        a2_ref[:, seg] = (gates[:MLSTM_HEADS, seg] - b) * LOG2E
    f2_ref[...] = lf * LOG2E


def _mlstm_pre(h, g, wqk, wv, wog, wg, cw, cb, gb, hg, tm, seq, lc):
    m, d = h.shape
    hq = MLSTM_HEADS * MLSTM_DK
    hv = MLSTM_HEADS * MLSTM_DV
    row = lambda i: (i, 0)
    colblk = lambda i: (0, i)
    halo = lambda i: (jnp.maximum(i * (tm // CONV_HALO) - 1, 0), 0)
    return pl.pallas_call(
        functools.partial(_mlstm_pre_kernel, tiles_per_seq=seq // tm, lc=lc),
        out_shape=(jax.ShapeDtypeStruct((m, hq), BF16), jax.ShapeDtypeStruct((hq, m), BF16),
                   jax.ShapeDtypeStruct((m, hv), BF16), jax.ShapeDtypeStruct((m, hv), BF16),
                   jax.ShapeDtypeStruct((MLSTM_HEADS, m), F32), jax.ShapeDtypeStruct((MLSTM_HEADS, m), F32)),
        grid=(m // tm,),
        in_specs=[pl.BlockSpec((tm, d), row), pl.BlockSpec((CONV_HALO, d), halo), _const_spec(g.shape),
                  _const_spec(wqk.shape), _const_spec(wv.shape), _const_spec(wog.shape), _const_spec(wg.shape),
                  _const_spec(cw.shape), _const_spec(cb.shape), _const_spec(gb.shape), _const_spec(hg.shape)],
        out_specs=(pl.BlockSpec((tm, hq), row), pl.BlockSpec((hq, tm), colblk), pl.BlockSpec((tm, hv), row),
                   pl.BlockSpec((tm, hv), row), pl.BlockSpec((MLSTM_HEADS, tm), colblk),
                   pl.BlockSpec((MLSTM_HEADS, tm), colblk)),
        scratch_shapes=[pltpu.VMEM((2 * hq // LANES, CONV_HALO + tm, LANES), F32)],
        compiler_params=_params("parallel"),
        name="mlstm_pre",
    )(h, h, g, wqk, wv, wog, wg, cw, cb, gb, hg)


def _mlstm_chunk_kernel(q_ref, kt_ref, v_ref, og_ref, a2_ref, f2_ref, o_ref, ct_ref, m_ref):
    lc = q_ref.shape[0]
    pair = 2 * MLSTM_DK

    @pl.when(pl.program_id(1) == 0)
    def _():
        ct_ref[...] = jnp.zeros_like(ct_ref)
        m_ref[...] = jnp.zeros_like(m_ref)

    causal = lax.broadcasted_iota(jnp.int32, (lc, lc), 0) >= lax.broadcasted_iota(jnp.int32, (lc, lc), 1)
    a2 = a2_ref[...]
    f2 = f2_ref[...]
    m2_prev = m_ref[...]
    a_max = jnp.max(a2, axis=1, keepdims=True)
    m_last = jnp.maximum(m2_prev, a_max)
    m_ref[...] = jnp.sum(f2, axis=1, keepdims=True) + m_last
    a_st = jnp.exp2(m2_prev - m_last)
    w_st = jnp.exp2(a2 - jnp.maximum(m2_prev[:, 0:1], a_max)).astype(BF16)

    lane = lax.broadcasted_iota(jnp.int32, (lc, pair), 1)
    ones = jnp.ones((lc, MLSTM_DV), BF16)
    top = lax.broadcasted_iota(jnp.int32, (pair, MLSTM_DV), 0) < MLSTM_DK

    top2 = jnp.concatenate([top, top], axis=1)
    qms, raws, vaugs, ct_bs = [], [], [], []
    for p in range(MLSTM_HEADS // 2):
        qb = q_ref[:, p * pair:(p + 1) * pair]
        ktb = kt_ref[p * pair:(p + 1) * pair, :]
        ct = ct_ref[p]
        ct_bs.append(ct.astype(BF16))
        upd = []
        for hh in range(2):
            h = 2 * p + hh
            qm = jnp.where((lane < MLSTM_DK) if hh == 0 else (lane >= MLSTM_DK), qb, jnp.zeros_like(qb))
            vaug = jnp.concatenate([v_ref[:, h * MLSTM_DV:(h + 1) * MLSTM_DV], ones], axis=1)
            qms.append(qm)
            vaugs.append(vaug)
            raws.append(_dot(qm, ktb))
            upd.append(_dot(ktb * w_st[h:h + 1, :], vaug))
        decay = jnp.where(top, a_st[2 * p:2 * p + 1, :], a_st[2 * p + 1:2 * p + 2, :])
        ct_ref[p] = jnp.concatenate([decay, decay], axis=1) * ct + jnp.where(top2, upd[0], upd[1])

    for h in range(MLSTM_HEADS):
        qm = qms[h]
        d = jnp.where(causal, a2[h:h + 1, :], -jnp.inf)
        m_h = jnp.broadcast_to(m2_prev[h:h + 1, :], (lc, MLSTM_DV))
        m_col = jnp.maximum(m_h, jnp.max(d, axis=1, keepdims=True))
        b_col = jnp.sum(jnp.where(causal, f2[h:h + 1, :], 0.0), axis=1, keepdims=True)
        s = raws[h] * jnp.exp2(d - jnp.concatenate([m_col] * (lc // MLSTM_DV), axis=1))
        q_inter = (qm.astype(F32) * jnp.exp2(m_h - m_col)).astype(BF16)
        nd = _dot(jnp.concatenate([s.astype(BF16), q_inter], axis=1),
                  jnp.concatenate([vaugs[h], ct_bs[h // 2]], axis=0))
        den = jnp.maximum(jnp.abs(nd[:, MLSTM_DV:]), jnp.exp2(-(b_col + m_col)))
        hv = nd[:, :MLSTM_DV] / den
        hn = hv * lax.rsqrt(jnp.mean(hv * hv, axis=-1, keepdims=True) + EPS)
        vsl = slice(h * MLSTM_DV, (h + 1) * MLSTM_DV)
        o_ref[:, vsl] = og_ref[:, vsl] * hn.astype(BF16)


def _mlstm_chunk(q, kt, v, og, a2, f2, batch, seq, lc):
    m = q.shape[0]
    hq = MLSTM_HEADS * MLSTM_DK
    hv = MLSTM_HEADS * MLSTM_DV
    nc = seq // lc
    row = lambda b, c: (b * nc + c, 0)
    colblk = lambda b, c: (0, b * nc + c)
    return pl.pallas_call(
        _mlstm_chunk_kernel,
        out_shape=jax.ShapeDtypeStruct((m, hv), BF16),
        grid=(batch, nc),
        in_specs=[pl.BlockSpec((lc, hq), row), pl.BlockSpec((hq, lc), colblk), pl.BlockSpec((lc, hv), row),
                  pl.BlockSpec((lc, hv), row), pl.BlockSpec((MLSTM_HEADS, lc), colblk),
                  pl.BlockSpec((MLSTM_HEADS, lc), colblk)],
        out_specs=pl.BlockSpec((lc, hv), row),
        scratch_shapes=[pltpu.VMEM((MLSTM_HEADS // 2, 2 * MLSTM_DK, 2 * MLSTM_DV), F32),
                        pltpu.VMEM((MLSTM_HEADS, LANES), F32)],
        compiler_params=_params("parallel", "arbitrary"),
        name="mlstm_chunk",
    )(q, kt, v, og, a2, f2)


def _head_block(nope, x1, x2):
    z = jnp.zeros(nope.shape[:-1] + (HEAD_BLOCK - MLA_NOPE - MLA_ROPE,), nope.dtype)
    return jnp.concatenate([nope[..., :48], x1, nope[..., 48:], z, x2], axis=-1)


def _mla_weights(w_in, w_uq, w_ukv):
    d = w_in.shape[0]
    half = MLA_ROPE // 2
    kr = w_in[:, MLA_Q_RANK + MLA_KV_RANK:]
    kr_blk = _head_block(jnp.zeros((d, MLA_NOPE), w_in.dtype), kr[:, :half], kr[:, half:])
    win = jnp.concatenate([w_in[:, :MLA_Q_RANK + MLA_KV_RANK], kr_blk], axis=1).astype(BF16)
    uq = w_uq.reshape(MLA_Q_RANK, MLA_HEADS, MLA_QK)
    wuq = _head_block(uq[..., :MLA_NOPE], uq[..., MLA_NOPE:MLA_NOPE + half], uq[..., MLA_NOPE + half:])
    wuq = wuq.reshape(MLA_Q_RANK, MLA_HEADS * HEAD_BLOCK).astype(BF16)
    ukv = w_ukv.reshape(MLA_KV_RANK, MLA_HEADS, MLA_NOPE + MLA_V)
    wuk = ukv[..., :MLA_NOPE].reshape(MLA_KV_RANK, MLA_HEADS * MLA_NOPE).T.astype(BF16)
    wuv = ukv[..., MLA_NOPE:].reshape(MLA_KV_RANK, MLA_HEADS * MLA_V).astype(BF16)
    return win, wuq, wuk, wuv


def _rope_inv_pattern():
    inv = ROPE_THETA ** (-jnp.arange(0, MLA_ROPE, 2, dtype=F32) / MLA_ROPE)
    z48 = jnp.zeros((48,), F32)
    return jnp.concatenate([z48, -inv, z48, inv]).reshape(1, LANES)


def kernel(x, positions, norm_mix_g, norm_mlp_g, final_g, mla_w_in, mla_q_norm_g, mla_kv_norm_g, mla_w_uq,
           mla_w_ukv, mla_w_o, mlstm_w_in, mlstm_conv_w, mlstm_conv_b, mlstm_i_b, mlstm_f_b,
           mlstm_head_norm_g, mlstm_w_o, mlp_w1, mlp_w2):
    batch, seq, d = x.shape
    m = batch * seq
    depth = norm_mix_g.shape[0]
    tm = min(512, seq)
    tq = min(256, seq)
    lc = min(256, seq)
    assert seq % tm == 0 and seq % tq == 0 and tm % lc == 0 and tm % CONV_HALO == 0

    h = x.reshape(m, d)
    cosp, sinp = _rope_tables(positions.reshape(m, 1), _rope_inv_pattern(), min(2048, m))
    q_scale = (MLA_QK ** -0.5) * math.log2(math.e)
    row = lambda v: v.reshape(1, -1)
    hq = MLSTM_HEADS * MLSTM_DK
    hv = MLSTM_HEADS * MLSTM_DV

    for layer in range(depth):
        j = layer // 2
        g_mix = row(norm_mix_g[layer])
        if layer % 2 == 0:
            win, wuq, wuk, wuv = _mla_weights(mla_w_in[j], mla_w_uq[j], mla_w_ukv[j])
            q, kt, v = _mla_pre(h, g_mix, win, row(mla_q_norm_g[j]), row(mla_kv_norm_g[j]), wuq, wuk, wuv,
                               cosp, sinp, tm, q_scale)
            mixed = _attention(q, kt, v, batch, seq, tq)
            w_o = mla_w_o[j]
        else:
            w = mlstm_w_in[j]
            wqk = w[:, :2 * hq].astype(BF16)
            wv = w[:, 2 * hq:2 * hq + hv].astype(BF16)
            wog = w[:, 2 * hq + hv:2 * hq + 2 * hv].astype(BF16)
            wg = w[:, 2 * hq + 2 * hv:].T.astype(BF16)
            gb = jnp.concatenate([mlstm_i_b[j], mlstm_f_b[j]]).reshape(2 * MLSTM_HEADS, 1)
            q, kt, v, og, a2, f2 = _mlstm_pre(h, g_mix, wqk, wv, wog, wg, mlstm_conv_w[j], row(mlstm_conv_b[j]), gb,
                                              row(mlstm_head_norm_g[j]), tm, seq, lc)
            mixed = _mlstm_chunk(q, kt, v, og, a2, f2, batch, seq, lc)
            w_o = mlstm_w_o[j]
        h = _post(mixed, w_o.astype(BF16), h, row(norm_mlp_g[layer]), mlp_w1[layer].astype(BF16),
                  mlp_w2[layer].astype(BF16), row(final_g), tm, final=(layer == depth - 1))
    return h.reshape(batch, seq, d)
```

```python
import functools
import math

import jax
import jax.numpy as jnp
from jax import lax
from jax.experimental import pallas as pl
from jax.experimental.pallas import tpu as pltpu

EPS = 1e-6
ROPE_THETA = 10000.0
GATE_SOFTCAP = 15.0

MLA_HEADS = 16
MLA_NOPE = 64
MLA_ROPE = 32
MLA_V = 64
MLA_QK = MLA_NOPE + MLA_ROPE
MLA_Q_RANK = 512
MLA_KV_RANK = 256

MLSTM_HEADS = 8
MLSTM_DK = 64
MLSTM_DV = 128
MLSTM_CONV = 4

LANES = 128
HEAD_BLOCK = LANES
KV_BLOCK = LANES
ATTN_ROWS = 32
CONV_HALO = 16
VMEM_LIMIT = 56 * 1024 * 1024
LOG2E = math.log2(math.e)

BF16 = jnp.bfloat16
F32 = jnp.float32


def _rms(x, g):
    return x * lax.rsqrt(jnp.mean(x * x, axis=-1, keepdims=True) + EPS) * g


def _dot(a, b):
    return jnp.dot(a, b, preferred_element_type=F32)


def _sigmoid(x):
    return 1.0 / (1.0 + jnp.exp2(x * (-LOG2E)))


def _params(*sem):
    return pltpu.CompilerParams(dimension_semantics=sem, vmem_limit_bytes=VMEM_LIMIT)


def _const_spec(shape):
    return pl.BlockSpec(shape, lambda *_: (0,) * len(shape), pipeline_mode=pl.Buffered(1))


def _rope_kernel(pos_ref, inv_ref, cos_ref, sin_ref):
    ang = pos_ref[...].astype(F32) * inv_ref[...]
    cos_ref[...] = jnp.cos(ang)
    sin_ref[...] = jnp.sin(ang)


def _rope_tables(pos_col, inv_pat, tr):
    m = pos_col.shape[0]
    return pl.pallas_call(
        _rope_kernel,
        out_shape=(jax.ShapeDtypeStruct((m, LANES), F32),) * 2,
        grid=(m // tr,),
        in_specs=[pl.BlockSpec((tr, 1), lambda i: (i, 0)), _const_spec((1, LANES))],
        out_specs=(pl.BlockSpec((tr, LANES), lambda i: (i, 0)),) * 2,
        compiler_params=_params("parallel"),
        name="rope_tables",
    )(pos_col, inv_pat)


def _mla_pre_kernel(h_ref, g_ref, win_ref, gq_ref, gkv_ref, wuq_ref, wukt_ref, wuv_ref,
                    cos_ref, sin_ref, q_ref, kt_ref, v_ref, *, q_scale):
    u = _rms(h_ref[...], g_ref[...]).astype(BF16)
    proj = _dot(u, win_ref[...])
    cq = _rms(proj[:, :MLA_Q_RANK], gq_ref[...]).astype(BF16)
    ckv = _rms(proj[:, MLA_Q_RANK:MLA_Q_RANK + MLA_KV_RANK], gkv_ref[...]).astype(BF16)
    kr = proj[:, MLA_Q_RANK + MLA_KV_RANK:]
    c = cos_ref[...]
    s = sin_ref[...]
    half = HEAD_BLOCK // 2
    krt = (kr * c + pltpu.roll(kr, half, axis=1) * s).T
    q = _dot(cq, wuq_ref[...])
    knt = lax.dot_general(wukt_ref[...], ckv, (((1,), (1,)), ((), ())), preferred_element_type=F32)
    cs = c * q_scale
    ss = s * q_scale
    zpad = jnp.zeros((HEAD_BLOCK - MLA_NOPE - MLA_ROPE, krt.shape[1]), F32)
    for h in range(MLA_HEADS):
        sl = slice(h * HEAD_BLOCK, (h + 1) * HEAD_BLOCK)
        qh = q[:, sl]
        q_ref[:, sl] = (qh * cs + pltpu.roll(qh, half, axis=1) * ss).astype(BF16)
        nh = knt[h * MLA_NOPE:(h + 1) * MLA_NOPE, :]
        kt_ref[sl, :] = jnp.concatenate([nh[:48], krt[48:64], nh[48:], zpad, krt[112:]], axis=0).astype(BF16)
    v_ref[...] = _dot(ckv, wuv_ref[...]).astype(BF16)


def _mla_pre(h, g, win, gq, gkv, wuq, wuk, wuv, cosp, sinp, tm, q_scale):
    m, d = h.shape
    hq = MLA_HEADS * HEAD_BLOCK
    hv = MLA_HEADS * MLA_V
    row = lambda i: (i, 0)
    return pl.pallas_call(
        functools.partial(_mla_pre_kernel, q_scale=q_scale),
        out_shape=(jax.ShapeDtypeStruct((m, hq), BF16), jax.ShapeDtypeStruct((hq, m), BF16),
                   jax.ShapeDtypeStruct((m, hv), BF16)),
        grid=(m // tm,),
        in_specs=[pl.BlockSpec((tm, d), row), _const_spec(g.shape), _const_spec(win.shape),
                  _const_spec(gq.shape), _const_spec(gkv.shape), _const_spec(wuq.shape),
                  _const_spec(wuk.shape), _const_spec(wuv.shape),
                  pl.BlockSpec((tm, LANES), row), pl.BlockSpec((tm, LANES), row)],
        out_specs=(pl.BlockSpec((tm, hq), row), pl.BlockSpec((hq, tm), lambda i: (0, i)),
                   pl.BlockSpec((tm, hv), row)),
        compiler_params=_params("parallel"),
        name="mla_pre",
    )(h, g, win, gq, gkv, wuq, wuk, wuv, cosp, sinp)


def _attn_kernel(q_ref, kt_ref, v_ref, o_ref, kbd_ref, vbd_ref, s_ref, p_ref, *, tq):
    seq = q_ref.shape[0]
    nkb = seq // KV_BLOCK
    pw = 2 * KV_BLOCK

    vw = 2 * MLA_V
    lane_v = lax.broadcasted_iota(jnp.int32, (KV_BLOCK, vw), 1)
    first_head = lane_v < MLA_V

    @pl.when((pl.program_id(0) == 0) & (pl.program_id(1) == 0))
    def _():
        kbd_ref[...] = jnp.zeros_like(kbd_ref)
        for j in range(nkb):
            vbd_ref[j * pw:j * pw + KV_BLOCK, vw:] = jnp.where(first_head, 1.0, 0.0).astype(BF16)
            vbd_ref[j * pw + KV_BLOCK:(j + 1) * pw, vw:] = jnp.where(first_head, 0.0, 1.0).astype(BF16)

    def build(j):
        ks = slice(j * KV_BLOCK, (j + 1) * KV_BLOCK)
        kbd_ref[:HEAD_BLOCK, j * pw:j * pw + KV_BLOCK] = kt_ref[:HEAD_BLOCK, ks]
        kbd_ref[HEAD_BLOCK:, j * pw + KV_BLOCK:(j + 1) * pw] = kt_ref[HEAD_BLOCK:, ks]
        vb = v_ref[ks, :]
        vbd_ref[j * pw:j * pw + KV_BLOCK, :vw] = jnp.where(first_head, vb, jnp.zeros_like(vb))
        vbd_ref[j * pw + KV_BLOCK:(j + 1) * pw, :vw] = jnp.where(first_head, jnp.zeros_like(vb), vb)

    row = lax.broadcasted_iota(jnp.int32, (tq, pw), 0)
    col = lax.broadcasted_iota(jnp.int32, (tq, pw), 1)
    rel = row - jnp.where(col < KV_BLOCK, col, col - KV_BLOCK)

    def scores(qi):
        q0 = qi * tq
        for j in range(q0 // KV_BLOCK, (q0 + tq) // KV_BLOCK):
            build(j)
        qv = q_ref[q0:q0 + tq, :]
        for j in range((q0 + tq) // KV_BLOCK):
            blk = _dot(qv, kbd_ref[:, j * pw:(j + 1) * pw])
            if (j + 1) * KV_BLOCK - 1 > q0:
                blk = jnp.where(rel >= j * KV_BLOCK - q0, blk, -jnp.inf)
            s_ref[qi % 2, :, j * pw:(j + 1) * pw] = blk

    def head_rows(x, fn):
        return jnp.concatenate([jnp.broadcast_to(fn(x[:, hh * KV_BLOCK:(hh + 1) * KV_BLOCK], axis=-1, keepdims=True),
                                                 (x.shape[0], KV_BLOCK)) for hh in range(2)], axis=1)

    nq = seq // tq
    scores(0)
    for qi in range(nq):
        q0 = qi * tq
        nb = (q0 + tq) // KV_BLOCK
        slot = qi % 2
        if qi + 1 < nq:
            scores(qi + 1)
        for r in range(tq // ATTN_ROWS):
            rows = slice(r * ATTN_ROWS, (r + 1) * ATTN_ROWS)
            nbr = (q0 + (r + 1) * ATTN_ROWS - 1) // KV_BLOCK + 1
            mx = functools.reduce(jnp.maximum, [s_ref[slot, rows, j * pw:(j + 1) * pw] for j in range(nbr)])
            m = head_rows(mx, jnp.max)
            for j in range(nb):
                if j < nbr:
                    p_ref[slot, rows, j * pw:(j + 1) * pw] = jnp.exp2(s_ref[slot, rows, j * pw:(j + 1) * pw] - m).astype(BF16)
                else:
                    p_ref[slot, rows, j * pw:(j + 1) * pw] = jnp.zeros((ATTN_ROWS, pw), BF16)
        o = _dot(p_ref[slot, :, :nb * pw], vbd_ref[:nb * pw, :])
        o_ref[q0:q0 + tq, :] = (o[:, :vw] / o[:, vw:]).astype(o_ref.dtype)


def _attention(q, kt, v, batch, seq, tq):
    m = q.shape[0]
    pairs = MLA_HEADS // 2
    nkb = seq // KV_BLOCK
    blk = lambda b, p: (b, p)
    return pl.pallas_call(
        functools.partial(_attn_kernel, tq=tq),
        out_shape=jax.ShapeDtypeStruct((m, MLA_HEADS * MLA_V), BF16),
        grid=(batch, pairs),
        in_specs=[pl.BlockSpec((seq, 2 * HEAD_BLOCK), blk),
                  pl.BlockSpec((2 * HEAD_BLOCK, seq), lambda b, p: (p, b)),
                  pl.BlockSpec((seq, 2 * MLA_V), blk)],
        out_specs=pl.BlockSpec((seq, 2 * MLA_V), blk),
        scratch_shapes=[pltpu.VMEM((2 * HEAD_BLOCK, nkb * 2 * KV_BLOCK), BF16),
                        pltpu.VMEM((nkb * 2 * KV_BLOCK, 4 * MLA_V), BF16),
                        pltpu.VMEM((2, tq, nkb * 2 * KV_BLOCK), F32),
                        pltpu.VMEM((2, tq, nkb * 2 * KV_BLOCK), BF16)],
        compiler_params=_params("arbitrary", "arbitrary"),
        name="mla_attention",
    )(q, kt, v)


def _post_kernel(a_ref, wo_ref, h_ref, g_ref, w1_ref, w2_ref, gf_ref, o_ref, *, hidden_chunk, final):
    h1 = h_ref[...] + _dot(a_ref[...], wo_ref[...])
    u = _rms(h1, g_ref[...]).astype(BF16)
    acc = h1
    hidden = w1_ref.shape[1]
    for c in range(hidden // hidden_chunk):
        sl = slice(c * hidden_chunk, (c + 1) * hidden_chunk)
        a = jnp.maximum(_dot(u, w1_ref[:, sl]), 0.0)
        acc = acc + _dot((a * a).astype(BF16), w2_ref[sl, :])
    if final:
        acc = _rms(acc, gf_ref[...])
    o_ref[...] = acc


def _post(a, wo, h, g, w1, w2, gf, tm, final):
    m, d = h.shape
    row = lambda i: (i, 0)
    return pl.pallas_call(
        functools.partial(_post_kernel, hidden_chunk=1024, final=final),
        out_shape=jax.ShapeDtypeStruct((m, d), F32),
        grid=(m // tm,),
        in_specs=[pl.BlockSpec((tm, a.shape[1]), row), _const_spec(wo.shape), pl.BlockSpec((tm, d), row),
                  _const_spec(g.shape), _const_spec(w1.shape), _const_spec(w2.shape), _const_spec(gf.shape)],
        out_specs=pl.BlockSpec((tm, d), row),
        compiler_params=_params("parallel"),
        name="post_mlp",
    )(a, wo, h, g, w1, w2, gf)


def _split3(x):
    x1 = x.astype(BF16)
    r1 = x - x1.astype(F32)
    x2 = r1.astype(BF16)
    x3 = (r1 - x2.astype(F32)).astype(BF16)
    return x1, x2, x3


def _mlstm_pre_kernel(h_ref, halo_ref, g_ref, wqk_ref, wv_ref, wog_ref, wg_ref, cw_ref, cb_ref, gb_ref, hg_ref,
                      q_ref, kt_ref, v_ref, og_ref, a2_ref, f2_ref, conv_ref, *, tiles_per_seq, lc):
    tm = h_ref.shape[0]
    g = g_ref[...]
    u = _rms(h_ref[...], g).astype(BF16)
    first = pl.program_id(0) % tiles_per_seq == 0
    uh = jnp.where(first, 0.0, _rms(halo_ref[...], g)).astype(BF16)
    qk = _dot(jnp.concatenate([uh, u], axis=0), wqk_ref[...])
    nslab = qk.shape[1] // LANES
    for c in range(nslab):
        conv_ref[c] = qk[:, c * LANES:(c + 1) * LANES]
    v_ref[...] = _dot(u, wv_ref[...]).astype(BF16)
    o_pre = _dot(u, wog_ref[...])
    gates = lax.dot_general(wg_ref[...], u, (((1,), (1,)), ((), ())), preferred_element_type=F32) + gb_ref[...]
    cw = cw_ref[...]
    cb = cb_ref[...]
    hq = MLSTM_HEADS * MLSTM_DK
    for c in range(nslab):
        sl = slice(c * LANES, (c + 1) * LANES)
        y = cb[:, sl]
        for j in range(MLSTM_CONV):
            off = CONV_HALO - (MLSTM_CONV - 1) + j
            y = y + conv_ref[c, off:off + tm, :] * cw[j:j + 1, sl]
        y = y * _sigmoid(y)
        if c * LANES < hq:
            q_ref[:, sl] = y.astype(BF16)
        else:
            kt_ref[c * LANES - hq:(c + 1) * LANES - hq, :] = (y * (MLSTM_DK ** -0.5)).T.astype(BF16)
    og_ref[...] = (_sigmoid(o_pre) * hg_ref[...]).astype(BF16)
    gates = GATE_SOFTCAP * jnp.tanh(gates * (1.0 / GATE_SOFTCAP))
    z = gates[MLSTM_HEADS:]
    lf = -(jnp.maximum(-z, 0.0) + jnp.log1p(jnp.exp(-jnp.abs(z))))
    upper = (lax.broadcasted_iota(jnp.int32, (lc, lc), 0) <= lax.broadcasted_iota(jnp.int32, (lc, lc), 1))
    upper = jnp.where(upper, 1.0, 0.0).astype(BF16)
    for c in range(tm // lc):
        seg = slice(c * lc, (c + 1) * lc)
        b = sum(_dot(piece, upper) for piece in _split3(lf[:, seg]))
        a2_ref[:, seg] = (gates[:MLSTM_HEADS, seg] - b) * LOG2E
    f2_ref[...] = lf * LOG2E


def _mlstm_pre(h, g, wqk, wv, wog, wg, cw, cb, gb, hg, tm, seq, lc):
    m, d = h.shape
    hq = MLSTM_HEADS * MLSTM_DK
    hv = MLSTM_HEADS * MLSTM_DV
    row = lambda i: (i, 0)
    colblk = lambda i: (0, i)
    halo = lambda i: (jnp.maximum(i * (tm // CONV_HALO) - 1, 0), 0)
    return pl.pallas_call(
        functools.partial(_mlstm_pre_kernel, tiles_per_seq=seq // tm, lc=lc),
        out_shape=(jax.ShapeDtypeStruct((m, hq), BF16), jax.ShapeDtypeStruct((hq, m), BF16),
                   jax.ShapeDtypeStruct((m, hv), BF16), jax.ShapeDtypeStruct((m, hv), BF16),
                   jax.ShapeDtypeStruct((MLSTM_HEADS, m), F32), jax.ShapeDtypeStruct((MLSTM_HEADS, m), F32)),
        grid=(m // tm,),
        in_specs=[pl.BlockSpec((tm, d), row), pl.BlockSpec((CONV_HALO, d), halo), _const_spec(g.shape),
                  _const_spec(wqk.shape), _const_spec(wv.shape), _const_spec(wog.shape), _const_spec(wg.shape),
                  _const_spec(cw.shape), _const_spec(cb.shape), _const_spec(gb.shape), _const_spec(hg.shape)],
        out_specs=(pl.BlockSpec((tm, hq), row), pl.BlockSpec((hq, tm), colblk), pl.BlockSpec((tm, hv), row),
                   pl.BlockSpec((tm, hv), row), pl.BlockSpec((MLSTM_HEADS, tm), colblk),
                   pl.BlockSpec((MLSTM_HEADS, tm), colblk)),
        scratch_shapes=[pltpu.VMEM((2 * hq // LANES, CONV_HALO + tm, LANES), F32)],
        compiler_params=_params("parallel"),
        name="mlstm_pre",
    )(h, h, g, wqk, wv, wog, wg, cw, cb, gb, hg)


def _mlstm_chunk_kernel(*refs, nseq):
    n_in = 6
    ins = [refs[n_in * r:n_in * (r + 1)] for r in range(nseq)]
    o_ref, ct_ref, m_ref = refs[n_in * nseq:]
    lc = ins[0][0].shape[0]
    pair = 2 * MLSTM_DK
    pairs = MLSTM_HEADS // 2

    @pl.when(pl.program_id(1) == 0)
    def _():
        ct_ref[...] = jnp.zeros_like(ct_ref)
        m_ref[...] = jnp.zeros_like(m_ref)

    causal = lax.broadcasted_iota(jnp.int32, (lc, lc), 0) >= lax.broadcasted_iota(jnp.int32, (lc, lc), 1)
    lane = lax.broadcasted_iota(jnp.int32, (lc, pair), 1)
    ones = jnp.ones((lc, MLSTM_DV), BF16)
    top = lax.broadcasted_iota(jnp.int32, (pair, MLSTM_DV), 0) < MLSTM_DK
    top2 = jnp.concatenate([top, top], axis=1)

    def state_independent(r):
        q_ref, kt_ref, v_ref, _, a2_ref, f2_ref = ins[r]
        a2 = a2_ref[...]
        f2 = f2_ref[...]
        mrows = slice(r * MLSTM_HEADS, (r + 1) * MLSTM_HEADS)
        m2_prev = m_ref[mrows, :]
        a_max = jnp.max(a2, axis=1, keepdims=True)
        m_last = jnp.maximum(m2_prev, a_max)
        m_ref[mrows, :] = jnp.sum(f2, axis=1, keepdims=True) + m_last
        a_st = jnp.exp2(m2_prev - m_last)
        w_st = jnp.exp2(a2 - jnp.maximum(m2_prev[:, 0:1], a_max)).astype(BF16)
        qms, raws, vaugs, ct_bs = [], [], [], []
        for p in range(pairs):
            qb = q_ref[:, p * pair:(p + 1) * pair]
            ktb = kt_ref[p * pair:(p + 1) * pair, :]
            ct = ct_ref[r * pairs + p]
            ct_bs.append(ct.astype(BF16))
            upd = []
            for hh in range(2):
                h = 2 * p + hh
                qm = jnp.where((lane < MLSTM_DK) if hh == 0 else (lane >= MLSTM_DK), qb, jnp.zeros_like(qb))
                vaug = jnp.concatenate([v_ref[:, h * MLSTM_DV:(h + 1) * MLSTM_DV], ones], axis=1)
                qms.append(qm)
                vaugs.append(vaug)
                raws.append(_dot(qm, ktb))
                upd.append(_dot(ktb * w_st[h:h + 1, :], vaug))
            decay = jnp.where(top, a_st[2 * p:2 * p + 1, :], a_st[2 * p + 1:2 * p + 2, :])
            ct_ref[r * pairs + p] = jnp.concatenate([decay, decay], axis=1) * ct + jnp.where(top2, upd[0], upd[1])
        return a2, f2, m2_prev, qms, raws, vaugs, ct_bs

    for r in range(nseq):
        a2, f2, m2_prev, qms, raws, vaugs, ct_bs = state_independent(r)
        og_ref = ins[r][3]
        for h in range(MLSTM_HEADS):
            qm = qms[h]
            d = jnp.where(causal, a2[h:h + 1, :], -jnp.inf)
            m_h = jnp.broadcast_to(m2_prev[h:h + 1, :], (lc, MLSTM_DV))
            m_col = jnp.maximum(m_h, jnp.max(d, axis=1, keepdims=True))
            b_col = jnp.sum(jnp.where(causal, f2[h:h + 1, :], 0.0), axis=1, keepdims=True)
            s = raws[h] * jnp.exp2(d - jnp.concatenate([m_col] * (lc // MLSTM_DV), axis=1))
            q_inter = (qm.astype(F32) * jnp.exp2(m_h - m_col)).astype(BF16)
            nd = _dot(jnp.concatenate([s.astype(BF16), q_inter], axis=1),
                      jnp.concatenate([vaugs[h], ct_bs[h // 2]], axis=0))
            den = jnp.maximum(jnp.abs(nd[:, MLSTM_DV:]), jnp.exp2(-(b_col + m_col)))
            hv = nd[:, :MLSTM_DV] / den
            hn = hv * lax.rsqrt(jnp.mean(hv * hv, axis=-1, keepdims=True) + EPS)
            vsl = slice(h * MLSTM_DV, (h + 1) * MLSTM_DV)
            o_ref[r, :, vsl] = og_ref[:, vsl] * hn.astype(BF16)


def _mlstm_chunk(q, kt, v, og, a2, f2, batch, seq, lc, nseq):
    m = q.shape[0]
    hq = MLSTM_HEADS * MLSTM_DK
    hv = MLSTM_HEADS * MLSTM_DV
    nc = seq // lc
    in_specs = []
    for r in range(nseq):
        row = lambda b, c, r=r: ((b * nseq + r) * nc + c, 0)
        colblk = lambda b, c, r=r: (0, (b * nseq + r) * nc + c)
        in_specs += [pl.BlockSpec((lc, hq), row), pl.BlockSpec((hq, lc), colblk), pl.BlockSpec((lc, hv), row),
                     pl.BlockSpec((lc, hv), row), pl.BlockSpec((MLSTM_HEADS, lc), colblk),
                     pl.BlockSpec((MLSTM_HEADS, lc), colblk)]
    out = pl.pallas_call(
        functools.partial(_mlstm_chunk_kernel, nseq=nseq),
        out_shape=jax.ShapeDtypeStruct((batch // nseq, nseq, seq, hv), BF16),
        grid=(batch // nseq, nc),
        in_specs=in_specs,
        out_specs=pl.BlockSpec((None, nseq, lc, hv), lambda b, c: (b, 0, c, 0)),
        scratch_shapes=[pltpu.VMEM((nseq * MLSTM_HEADS // 2, 2 * MLSTM_DK, 2 * MLSTM_DV), F32),
                        pltpu.VMEM((nseq * MLSTM_HEADS, LANES), F32)],
        compiler_params=_params("parallel", "arbitrary"),
        name="mlstm_chunk",
    )(*([q, kt, v, og, a2, f2] * nseq))
    return out.reshape(m, hv)


def _head_block(nope, x1, x2):
    z = jnp.zeros(nope.shape[:-1] + (HEAD_BLOCK - MLA_NOPE - MLA_ROPE,), nope.dtype)
    return jnp.concatenate([nope[..., :48], x1, nope[..., 48:], z, x2], axis=-1)


def _mla_weights(w_in, w_uq, w_ukv):
    d = w_in.shape[0]
    half = MLA_ROPE // 2
    kr = w_in[:, MLA_Q_RANK + MLA_KV_RANK:]
    kr_blk = _head_block(jnp.zeros((d, MLA_NOPE), w_in.dtype), kr[:, :half], kr[:, half:])
    win = jnp.concatenate([w_in[:, :MLA_Q_RANK + MLA_KV_RANK], kr_blk], axis=1).astype(BF16)
    uq = w_uq.reshape(MLA_Q_RANK, MLA_HEADS, MLA_QK)
    wuq = _head_block(uq[..., :MLA_NOPE], uq[..., MLA_NOPE:MLA_NOPE + half], uq[..., MLA_NOPE + half:])
    wuq = wuq.reshape(MLA_Q_RANK, MLA_HEADS * HEAD_BLOCK).astype(BF16)
    ukv = w_ukv.reshape(MLA_KV_RANK, MLA_HEADS, MLA_NOPE + MLA_V)
    wuk = ukv[..., :MLA_NOPE].reshape(MLA_KV_RANK, MLA_HEADS * MLA_NOPE).T.astype(BF16)
    wuv = ukv[..., MLA_NOPE:].reshape(MLA_KV_RANK, MLA_HEADS * MLA_V).astype(BF16)
    return win, wuq, wuk, wuv


def _rope_inv_pattern():
    inv = ROPE_THETA ** (-jnp.arange(0, MLA_ROPE, 2, dtype=F32) / MLA_ROPE)
    z48 = jnp.zeros((48,), F32)
    return jnp.concatenate([z48, -inv, z48, inv]).reshape(1, LANES)


def kernel(x, positions, norm_mix_g, norm_mlp_g, final_g, mla_w_in, mla_q_norm_g, mla_kv_norm_g, mla_w_uq,
           mla_w_ukv, mla_w_o, mlstm_w_in, mlstm_conv_w, mlstm_conv_b, mlstm_i_b, mlstm_f_b,
           mlstm_head_norm_g, mlstm_w_o, mlp_w1, mlp_w2):
    batch, seq, d = x.shape
    m = batch * seq
    depth = norm_mix_g.shape[0]
    tm = min(1024, seq)
    tq = min(256, seq)
    lc = min(256, seq)
    assert seq % tm == 0 and seq % tq == 0 and tm % lc == 0 and tm % CONV_HALO == 0

    h = x.reshape(m, d)
    cosp, sinp = _rope_tables(positions.reshape(m, 1), _rope_inv_pattern(), min(2048, seq))
    q_scale = (MLA_QK ** -0.5) * math.log2(math.e)
    row = lambda v: v.reshape(1, -1)
    hq = MLSTM_HEADS * MLSTM_DK
    hv = MLSTM_HEADS * MLSTM_DV

    for layer in range(depth):
        j = layer // 2
        g_mix = row(norm_mix_g[layer])
        if layer % 2 == 0:
            win, wuq, wuk, wuv = _mla_weights(mla_w_in[j], mla_w_uq[j], mla_w_ukv[j])
            q, kt, v = _mla_pre(h, g_mix, win, row(mla_q_norm_g[j]), row(mla_kv_norm_g[j]), wuq, wuk, wuv,
                               cosp, sinp, tm, q_scale)
            mixed = _attention(q, kt, v, batch, seq, tq)
            w_o = mla_w_o[j]
        else:
            w = mlstm_w_in[j]
            wqk = w[:, :2 * hq].astype(BF16)
            wv = w[:, 2 * hq:2 * hq + hv].astype(BF16)
            wog = w[:, 2 * hq + hv:2 * hq + 2 * hv].astype(BF16)
            wg = w[:, 2 * hq + 2 * hv:].T.astype(BF16)
            gb = jnp.concatenate([mlstm_i_b[j], mlstm_f_b[j]]).reshape(2 * MLSTM_HEADS, 1)
            q, kt, v, og, a2, f2 = _mlstm_pre(h, g_mix, wqk, wv, wog, wg, mlstm_conv_w[j], row(mlstm_conv_b[j]), gb,
                                              row(mlstm_head_norm_g[j]), tm, seq, lc)
            mixed = _mlstm_chunk(q, kt, v, og, a2, f2, batch, seq, lc, 2 if batch % 2 == 0 else 1)
            w_o = mlstm_w_o[j]
        h = _post(mixed, w_o.astype(BF16), h, row(norm_mlp_g[layer]), mlp_w1[layer].astype(BF16),
                  mlp_w2[layer].astype(BF16), row(final_g), tm, final=(layer == depth - 1))
    return h.reshape(batch, seq, d)
```

```python
import functools
import math

import jax
import jax.numpy as jnp
from jax import lax
from jax.experimental import pallas as pl
from jax.experimental.pallas import tpu as pltpu

EPS = 1e-6
ROPE_THETA = 10000.0
GATE_SOFTCAP = 15.0

MLA_HEADS = 16
MLA_NOPE = 64
MLA_ROPE = 32
MLA_V = 64
MLA_QK = MLA_NOPE + MLA_ROPE
MLA_Q_RANK = 512
MLA_KV_RANK = 256

MLSTM_HEADS = 8
MLSTM_DK = 64
MLSTM_DV = 128
MLSTM_CONV = 4

LANES = 128
HEAD_BLOCK = LANES
KV_BLOCK = LANES
ATTN_ROWS = 32
CONV_HALO = 16
VMEM_LIMIT = 56 * 1024 * 1024
LOG2E = math.log2(math.e)

BF16 = jnp.bfloat16
F32 = jnp.float32


def _rms(x, g):
    return x * lax.rsqrt(jnp.mean(x * x, axis=-1, keepdims=True) + EPS) * g


def _dot(a, b):
    return jnp.dot(a, b, preferred_element_type=F32)


def _sigmoid(x):
    return 1.0 / (1.0 + jnp.exp2(x * (-LOG2E)))


def _params(*sem):
    return pltpu.CompilerParams(dimension_semantics=sem, vmem_limit_bytes=VMEM_LIMIT)


def _const_spec(shape):
    return pl.BlockSpec(shape, lambda *_: (0,) * len(shape), pipeline_mode=pl.Buffered(1))


def _rope_kernel(pos_ref, inv_ref, cos_ref, sin_ref):
    ang = pos_ref[...].astype(F32) * inv_ref[...]
    cos_ref[...] = jnp.cos(ang)
    sin_ref[...] = jnp.sin(ang)


def _rope_tables(pos_col, inv_pat, tr):
    m = pos_col.shape[0]
    return pl.pallas_call(
        _rope_kernel,
        out_shape=(jax.ShapeDtypeStruct((m, LANES), F32),) * 2,
        grid=(m // tr,),
        in_specs=[pl.BlockSpec((tr, 1), lambda i: (i, 0)), _const_spec((1, LANES))],
        out_specs=(pl.BlockSpec((tr, LANES), lambda i: (i, 0)),) * 2,
        compiler_params=_params("parallel"),
        name="rope_tables",
    )(pos_col, inv_pat)


def _mla_pre_kernel(h_ref, g_ref, win_ref, gq_ref, gkv_ref, wuq_ref, wukt_ref, wuv_ref,
                    cos_ref, sin_ref, q_ref, kt_ref, v_ref, *, q_scale):
    u = _rms(h_ref[...], g_ref[...]).astype(BF16)
    proj = _dot(u, win_ref[...])
    cq = _rms(proj[:, :MLA_Q_RANK], gq_ref[...]).astype(BF16)
    ckv = _rms(proj[:, MLA_Q_RANK:MLA_Q_RANK + MLA_KV_RANK], gkv_ref[...]).astype(BF16)
    kr = proj[:, MLA_Q_RANK + MLA_KV_RANK:]
    c = cos_ref[...]
    s = sin_ref[...]
    half = HEAD_BLOCK // 2
    krt = (kr * c + pltpu.roll(kr, half, axis=1) * s).T
    q = _dot(cq, wuq_ref[...])
    knt = lax.dot_general(wukt_ref[...], ckv, (((1,), (1,)), ((), ())), preferred_element_type=F32)
    cs = c * q_scale
    ss = s * q_scale
    zpad = jnp.zeros((HEAD_BLOCK - MLA_NOPE - MLA_ROPE, krt.shape[1]), F32)
    for h in range(MLA_HEADS):
        sl = slice(h * HEAD_BLOCK, (h + 1) * HEAD_BLOCK)
        qh = q[:, sl]
        q_ref[:, sl] = (qh * cs + pltpu.roll(qh, half, axis=1) * ss).astype(BF16)
        nh = knt[h * MLA_NOPE:(h + 1) * MLA_NOPE, :]
        kt_ref[sl, :] = jnp.concatenate([nh[:48], krt[48:64], nh[48:], zpad, krt[112:]], axis=0).astype(BF16)
    v_ref[...] = _dot(ckv, wuv_ref[...]).astype(BF16)


def _mla_pre(h, g, win, gq, gkv, wuq, wuk, wuv, cosp, sinp, tm, q_scale):
    m, d = h.shape
    hq = MLA_HEADS * HEAD_BLOCK
    hv = MLA_HEADS * MLA_V
    row = lambda i: (i, 0)
    return pl.pallas_call(
        functools.partial(_mla_pre_kernel, q_scale=q_scale),
        out_shape=(jax.ShapeDtypeStruct((m, hq), BF16), jax.ShapeDtypeStruct((hq, m), BF16),
                   jax.ShapeDtypeStruct((m, hv), BF16)),
        grid=(m // tm,),
        in_specs=[pl.BlockSpec((tm, d), row), _const_spec(g.shape), _const_spec(win.shape),
                  _const_spec(gq.shape), _const_spec(gkv.shape), _const_spec(wuq.shape),
                  _const_spec(wuk.shape), _const_spec(wuv.shape),
                  pl.BlockSpec((tm, LANES), row), pl.BlockSpec((tm, LANES), row)],
        out_specs=(pl.BlockSpec((tm, hq), row), pl.BlockSpec((hq, tm), lambda i: (0, i)),
                   pl.BlockSpec((tm, hv), row)),
        compiler_params=_params("parallel"),
        name="mla_pre",
    )(h, g, win, gq, gkv, wuq, wuk, wuv, cosp, sinp)


def _attn_kernel(q_ref, kt_ref, v_ref, o_ref, kbd_ref, vbd_ref, s_ref, p_ref, *, tq):
    seq = q_ref.shape[0]
    nkb = seq // KV_BLOCK
    pw = 2 * KV_BLOCK

    vw = 2 * MLA_V
    lane_v = lax.broadcasted_iota(jnp.int32, (KV_BLOCK, vw), 1)
    first_head = lane_v < MLA_V

    @pl.when((pl.program_id(0) == 0) & (pl.program_id(1) == 0))
    def _():
        kbd_ref[...] = jnp.zeros_like(kbd_ref)
        for j in range(nkb):
            vbd_ref[j * pw:j * pw + KV_BLOCK, vw:] = jnp.where(first_head, 1.0, 0.0).astype(BF16)
            vbd_ref[j * pw + KV_BLOCK:(j + 1) * pw, vw:] = jnp.where(first_head, 0.0, 1.0).astype(BF16)

    def build(j):
        ks = slice(j * KV_BLOCK, (j + 1) * KV_BLOCK)
        kbd_ref[:HEAD_BLOCK, j * pw:j * pw + KV_BLOCK] = kt_ref[:HEAD_BLOCK, ks]
        kbd_ref[HEAD_BLOCK:, j * pw + KV_BLOCK:(j + 1) * pw] = kt_ref[HEAD_BLOCK:, ks]
        vb = v_ref[ks, :]
        vbd_ref[j * pw:j * pw + KV_BLOCK, :vw] = jnp.where(first_head, vb, jnp.zeros_like(vb))
        vbd_ref[j * pw + KV_BLOCK:(j + 1) * pw, :vw] = jnp.where(first_head, jnp.zeros_like(vb), vb)

    row = lax.broadcasted_iota(jnp.int32, (tq, pw), 0)
    col = lax.broadcasted_iota(jnp.int32, (tq, pw), 1)
    rel = row - jnp.where(col < KV_BLOCK, col, col - KV_BLOCK)

    def scores(qi):
        q0 = qi * tq
        for j in range(q0 // KV_BLOCK, (q0 + tq) // KV_BLOCK):
            build(j)
        for j in range((q0 + tq) // KV_BLOCK):
            r0 = max(0, j * KV_BLOCK - q0)
            blk = _dot(q_ref[q0 + r0:q0 + tq, :], kbd_ref[:, j * pw:(j + 1) * pw])
            if (j + 1) * KV_BLOCK - 1 > q0:
                blk = jnp.where(rel[r0:] >= j * KV_BLOCK - q0, blk, -jnp.inf)
            s_ref[qi % 2, r0:, j * pw:(j + 1) * pw] = blk

    def head_rows(x, fn):
        return jnp.concatenate([jnp.broadcast_to(fn(x[:, hh * KV_BLOCK:(hh + 1) * KV_BLOCK], axis=-1, keepdims=True),
                                                 (x.shape[0], KV_BLOCK)) for hh in range(2)], axis=1)

    def softmax(qi):
        q0 = qi * tq
        slot = qi % 2
        for r in range(tq // ATTN_ROWS):
            rows = slice(r * ATTN_ROWS, (r + 1) * ATTN_ROWS)
            nbr = (q0 + (r + 1) * ATTN_ROWS - 1) // KV_BLOCK + 1
            mx = functools.reduce(jnp.maximum, [s_ref[slot, rows, j * pw:(j + 1) * pw] for j in range(nbr)])
            m = head_rows(mx, jnp.max)
            for j in range(nbr):
                p_ref[slot, rows, j * pw:(j + 1) * pw] = jnp.exp2(s_ref[slot, rows, j * pw:(j + 1) * pw] - m).astype(BF16)

    def values(qi):
        q0 = qi * tq
        for g in range(tq // KV_BLOCK):
            rows = slice(g * KV_BLOCK, (g + 1) * KV_BLOCK)
            nbg = q0 // KV_BLOCK + g + 1
            o = _dot(p_ref[qi % 2, rows, :nbg * pw], vbd_ref[:nbg * pw, :])
            o_ref[q0 + g * KV_BLOCK:q0 + (g + 1) * KV_BLOCK, :] = (o[:, :vw] / o[:, vw:]).astype(o_ref.dtype)

    nq = seq // tq
    scores(0)
    for qi in range(nq):
        if qi + 1 < nq:
            scores(qi + 1)
        softmax(qi)
        if qi >= 1:
            values(qi - 1)
    values(nq - 1)


def _attention(q, kt, v, batch, seq, tq):
    m = q.shape[0]
    pairs = MLA_HEADS // 2
    nkb = seq // KV_BLOCK
    blk = lambda b, p: (b, p)
    return pl.pallas_call(
        functools.partial(_attn_kernel, tq=tq),
        out_shape=jax.ShapeDtypeStruct((m, MLA_HEADS * MLA_V), BF16),
        grid=(batch, pairs),
        in_specs=[pl.BlockSpec((seq, 2 * HEAD_BLOCK), blk),
                  pl.BlockSpec((2 * HEAD_BLOCK, seq), lambda b, p: (p, b)),
                  pl.BlockSpec((seq, 2 * MLA_V), blk)],
        out_specs=pl.BlockSpec((seq, 2 * MLA_V), blk),
        scratch_shapes=[pltpu.VMEM((2 * HEAD_BLOCK, nkb * 2 * KV_BLOCK), BF16),
                        pltpu.VMEM((nkb * 2 * KV_BLOCK, 4 * MLA_V), BF16),
                        pltpu.VMEM((2, tq, nkb * 2 * KV_BLOCK), F32),
                        pltpu.VMEM((2, tq, nkb * 2 * KV_BLOCK), BF16)],
        compiler_params=_params("arbitrary", "arbitrary"),
        name="mla_attention",
    )(q, kt, v)


def _post_kernel(a_ref, wo_ref, h_ref, g_ref, w1_ref, w2_ref, gf_ref, o_ref, *, hidden_chunk, final):
    h1 = h_ref[...] + _dot(a_ref[...], wo_ref[...])
    u = _rms(h1, g_ref[...]).astype(BF16)
    acc = h1
    hidden = w1_ref.shape[1]
    for c in range(hidden // hidden_chunk):
        sl = slice(c * hidden_chunk, (c + 1) * hidden_chunk)
        a = jnp.maximum(_dot(u, w1_ref[:, sl]), 0.0)
        acc = acc + _dot((a * a).astype(BF16), w2_ref[sl, :])
    if final:
        acc = _rms(acc, gf_ref[...])
    o_ref[...] = acc


def _post(a, wo, h, g, w1, w2, gf, tm, final):
    m, d = h.shape
    row = lambda i: (i, 0)
    return pl.pallas_call(
        functools.partial(_post_kernel, hidden_chunk=1024, final=final),
        out_shape=jax.ShapeDtypeStruct((m, d), F32),
        grid=(m // tm,),
        in_specs=[pl.BlockSpec((tm, a.shape[1]), row), _const_spec(wo.shape), pl.BlockSpec((tm, d), row),
                  _const_spec(g.shape), _const_spec(w1.shape), _const_spec(w2.shape), _const_spec(gf.shape)],
        out_specs=pl.BlockSpec((tm, d), row),
        compiler_params=_params("parallel"),
        name="post_mlp",
    )(a, wo, h, g, w1, w2, gf)


def _split3(x):
    x1 = x.astype(BF16)
    r1 = x - x1.astype(F32)
    x2 = r1.astype(BF16)
    x3 = (r1 - x2.astype(F32)).astype(BF16)
    return x1, x2, x3


def _mlstm_pre_kernel(h_ref, halo_ref, g_ref, wqk_ref, wv_ref, wog_ref, wg_ref, cw_ref, cb_ref, gb_ref, hg_ref,
                      q_ref, kt_ref, v_ref, og_ref, a2_ref, f2_ref, conv_ref, *, tiles_per_seq, lc):
    tm = h_ref.shape[0]
    g = g_ref[...]
    u = _rms(h_ref[...], g).astype(BF16)
    first = pl.program_id(0) % tiles_per_seq == 0
    uh = jnp.where(first, 0.0, _rms(halo_ref[...], g)).astype(BF16)
    qk = _dot(jnp.concatenate([uh, u], axis=0), wqk_ref[...])
    nslab = qk.shape[1] // LANES
    for c in range(nslab):
        conv_ref[c] = qk[:, c * LANES:(c + 1) * LANES]
    v_ref[...] = _dot(u, wv_ref[...]).astype(BF16)
    o_pre = _dot(u, wog_ref[...])
    gates = lax.dot_general(wg_ref[...], u, (((1,), (1,)), ((), ())), preferred_element_type=F32) + gb_ref[...]
    cw = cw_ref[...]
    cb = cb_ref[...]
    hq = MLSTM_HEADS * MLSTM_DK
    for c in range(nslab):
        sl = slice(c * LANES, (c + 1) * LANES)
        y = cb[:, sl]
        for j in range(MLSTM_CONV):
            off = CONV_HALO - (MLSTM_CONV - 1) + j
            y = y + conv_ref[c, off:off + tm, :] * cw[j:j + 1, sl]
        y = y * _sigmoid(y)
        if c * LANES < hq:
            q_ref[:, sl] = y.astype(BF16)
        else:
            kt_ref[c * LANES - hq:(c + 1) * LANES - hq, :] = (y * (MLSTM_DK ** -0.5)).T.astype(BF16)
    og_ref[...] = (_sigmoid(o_pre) * hg_ref[...]).astype(BF16)
    gates = GATE_SOFTCAP * jnp.tanh(gates * (1.0 / GATE_SOFTCAP))
    z = gates[MLSTM_HEADS:]
    lf = -(jnp.maximum(-z, 0.0) + jnp.log1p(jnp.exp(-jnp.abs(z))))
    upper = (lax.broadcasted_iota(jnp.int32, (lc, lc), 0) <= lax.broadcasted_iota(jnp.int32, (lc, lc), 1))
    upper = jnp.where(upper, 1.0, 0.0).astype(BF16)
    for c in range(tm // lc):
        seg = slice(c * lc, (c + 1) * lc)
        b = sum(_dot(piece, upper) for piece in _split3(lf[:, seg]))
        a2_ref[:, seg] = (gates[:MLSTM_HEADS, seg] - b) * LOG2E
    f2_ref[...] = lf * LOG2E


def _mlstm_pre(h, g, wqk, wv, wog, wg, cw, cb, gb, hg, tm, seq, lc):
    m, d = h.shape
    hq = MLSTM_HEADS * MLSTM_DK
    hv = MLSTM_HEADS * MLSTM_DV
    row = lambda i: (i, 0)
    colblk = lambda i: (0, i)
    halo = lambda i: (jnp.maximum(i * (tm // CONV_HALO) - 1, 0), 0)
    return pl.pallas_call(
        functools.partial(_mlstm_pre_kernel, tiles_per_seq=seq // tm, lc=lc),
        out_shape=(jax.ShapeDtypeStruct((m, hq), BF16), jax.ShapeDtypeStruct((hq, m), BF16),
                   jax.ShapeDtypeStruct((m, hv), BF16), jax.ShapeDtypeStruct((m, hv), BF16),
                   jax.ShapeDtypeStruct((MLSTM_HEADS, m), F32), jax.ShapeDtypeStruct((MLSTM_HEADS, m), F32)),
        grid=(m // tm,),
        in_specs=[pl.BlockSpec((tm, d), row), pl.BlockSpec((CONV_HALO, d), halo), _const_spec(g.shape),
                  _const_spec(wqk.shape), _const_spec(wv.shape), _const_spec(wog.shape), _const_spec(wg.shape),
                  _const_spec(cw.shape), _const_spec(cb.shape), _const_spec(gb.shape), _const_spec(hg.shape)],
        out_specs=(pl.BlockSpec((tm, hq), row), pl.BlockSpec((hq, tm), colblk), pl.BlockSpec((tm, hv), row),
                   pl.BlockSpec((tm, hv), row), pl.BlockSpec((MLSTM_HEADS, tm), colblk),
                   pl.BlockSpec((MLSTM_HEADS, tm), colblk)),
        scratch_shapes=[pltpu.VMEM((2 * hq // LANES, CONV_HALO + tm, LANES), F32)],
        compiler_params=_params("parallel"),
        name="mlstm_pre",
    )(h, h, g, wqk, wv, wog, wg, cw, cb, gb, hg)


def _mlstm_chunk_kernel(*refs, nseq):
    n_in = 6
    ins = [refs[n_in * r:n_in * (r + 1)] for r in range(nseq)]
    o_ref, ct_ref, m_ref = refs[n_in * nseq:]
    lc = ins[0][0].shape[0]
    pair = 2 * MLSTM_DK
    pairs = MLSTM_HEADS // 2

    @pl.when(pl.program_id(1) == 0)
    def _():
        ct_ref[...] = jnp.zeros_like(ct_ref)
        m_ref[...] = jnp.zeros_like(m_ref)

    causal = lax.broadcasted_iota(jnp.int32, (lc, lc), 0) >= lax.broadcasted_iota(jnp.int32, (lc, lc), 1)
    lane = lax.broadcasted_iota(jnp.int32, (lc, pair), 1)
    ones = jnp.ones((lc, MLSTM_DV), BF16)
    top = lax.broadcasted_iota(jnp.int32, (pair, MLSTM_DV), 0) < MLSTM_DK
    top2 = jnp.concatenate([top, top], axis=1)

    def state_independent(r):
        q_ref, kt_ref, v_ref, _, a2_ref, f2_ref = ins[r]
        a2 = a2_ref[...]
        f2 = f2_ref[...]
        mrows = slice(r * MLSTM_HEADS, (r + 1) * MLSTM_HEADS)
        m2_prev = m_ref[mrows, :]
        a_max = jnp.max(a2, axis=1, keepdims=True)
        m_last = jnp.maximum(m2_prev, a_max)
        m_ref[mrows, :] = jnp.sum(f2, axis=1, keepdims=True) + m_last
        a_st = jnp.exp2(m2_prev - m_last)
        w_st = jnp.exp2(a2 - jnp.maximum(m2_prev[:, 0:1], a_max)).astype(BF16)
        qms, raws, vaugs, ct_bs = [], [], [], []
        for p in range(pairs):
            qb = q_ref[:, p * pair:(p + 1) * pair]
            ktb = kt_ref[p * pair:(p + 1) * pair, :]
            ct = ct_ref[r * pairs + p]
            ct_bs.append(ct.astype(BF16))
            upd = []
            for hh in range(2):
                h = 2 * p + hh
                qm = jnp.where((lane < MLSTM_DK) if hh == 0 else (lane >= MLSTM_DK), qb, jnp.zeros_like(qb))
                vaug = jnp.concatenate([v_ref[:, h * MLSTM_DV:(h + 1) * MLSTM_DV], ones], axis=1)
                qms.append(qm)
                vaugs.append(vaug)
                raws.append(_dot(qm, ktb))
                upd.append(_dot(ktb * w_st[h:h + 1, :], vaug))
            decay = jnp.where(top, a_st[2 * p:2 * p + 1, :], a_st[2 * p + 1:2 * p + 2, :])
            ct_ref[r * pairs + p] = jnp.concatenate([decay, decay], axis=1) * ct + jnp.where(top2, upd[0], upd[1])
        return a2, f2, m2_prev, qms, raws, vaugs, ct_bs

    for r in range(nseq):
        a2, f2, m2_prev, qms, raws, vaugs, ct_bs = state_independent(r)
        og_ref = ins[r][3]
        for h in range(MLSTM_HEADS):
            qm = qms[h]
            d = jnp.where(causal, a2[h:h + 1, :], -jnp.inf)
            m_h = jnp.broadcast_to(m2_prev[h:h + 1, :], (lc, MLSTM_DV))
            m_col = jnp.maximum(m_h, jnp.max(d, axis=1, keepdims=True))
            b_col = jnp.sum(jnp.where(causal, f2[h:h + 1, :], 0.0), axis=1, keepdims=True)
            s = raws[h] * jnp.exp2(d - jnp.concatenate([m_col] * (lc // MLSTM_DV), axis=1))
            q_inter = (qm.astype(F32) * jnp.exp2(m_h - m_col)).astype(BF16)
            nd = _dot(jnp.concatenate([s.astype(BF16), q_inter], axis=1),
                      jnp.concatenate([vaugs[h], ct_bs[h // 2]], axis=0))
            den = jnp.maximum(jnp.abs(nd[:, MLSTM_DV:]), jnp.exp2(-(b_col + m_col)))
            hv = nd[:, :MLSTM_DV] / den
            hn = hv * lax.rsqrt(jnp.mean(hv * hv, axis=-1, keepdims=True) + EPS)
            vsl = slice(h * MLSTM_DV, (h + 1) * MLSTM_DV)
            o_ref[r, :, vsl] = og_ref[:, vsl] * hn.astype(BF16)


def _mlstm_chunk(q, kt, v, og, a2, f2, batch, seq, lc, nseq):
    m = q.shape[0]
    hq = MLSTM_HEADS * MLSTM_DK
    hv = MLSTM_HEADS * MLSTM_DV
    nc = seq // lc
    in_specs = []
    for r in range(nseq):
        row = lambda b, c, r=r: ((b * nseq + r) * nc + c, 0)
        colblk = lambda b, c, r=r: (0, (b * nseq + r) * nc + c)
        in_specs += [pl.BlockSpec((lc, hq), row), pl.BlockSpec((hq, lc), colblk), pl.BlockSpec((lc, hv), row),
                     pl.BlockSpec((lc, hv), row), pl.BlockSpec((MLSTM_HEADS, lc), colblk),
                     pl.BlockSpec((MLSTM_HEADS, lc), colblk)]
    out = pl.pallas_call(
        functools.partial(_mlstm_chunk_kernel, nseq=nseq),
        out_shape=jax.ShapeDtypeStruct((batch // nseq, nseq, seq, hv), BF16),
        grid=(batch // nseq, nc),
        in_specs=in_specs,
        out_specs=pl.BlockSpec((None, nseq, lc, hv), lambda b, c: (b, 0, c, 0)),
        scratch_shapes=[pltpu.VMEM((nseq * MLSTM_HEADS // 2, 2 * MLSTM_DK, 2 * MLSTM_DV), F32),
                        pltpu.VMEM((nseq * MLSTM_HEADS, LANES), F32)],
        compiler_params=_params("parallel", "arbitrary"),
        name="mlstm_chunk",
    )(*([q, kt, v, og, a2, f2] * nseq))
    return out.reshape(m, hv)


def _head_block(nope, x1, x2):
    z = jnp.zeros(nope.shape[:-1] + (HEAD_BLOCK - MLA_NOPE - MLA_ROPE,), nope.dtype)
    return jnp.concatenate([nope[..., :48], x1, nope[..., 48:], z, x2], axis=-1)


def _mla_weights(w_in, w_uq, w_ukv):
    d = w_in.shape[0]
    half = MLA_ROPE // 2
    kr = w_in[:, MLA_Q_RANK + MLA_KV_RANK:]
    kr_blk = _head_block(jnp.zeros((d, MLA_NOPE), w_in.dtype), kr[:, :half], kr[:, half:])
    win = jnp.concatenate([w_in[:, :MLA_Q_RANK + MLA_KV_RANK], kr_blk], axis=1).astype(BF16)
    uq = w_uq.reshape(MLA_Q_RANK, MLA_HEADS, MLA_QK)
    wuq = _head_block(uq[..., :MLA_NOPE], uq[..., MLA_NOPE:MLA_NOPE + half], uq[..., MLA_NOPE + half:])
    wuq = wuq.reshape(MLA_Q_RANK, MLA_HEADS * HEAD_BLOCK).astype(BF16)
    ukv = w_ukv.reshape(MLA_KV_RANK, MLA_HEADS, MLA_NOPE + MLA_V)
    wuk = ukv[..., :MLA_NOPE].reshape(MLA_KV_RANK, MLA_HEADS * MLA_NOPE).T.astype(BF16)
    wuv = ukv[..., MLA_NOPE:].reshape(MLA_KV_RANK, MLA_HEADS * MLA_V).astype(BF16)
    return win, wuq, wuk, wuv


def _rope_inv_pattern():
    inv = ROPE_THETA ** (-jnp.arange(0, MLA_ROPE, 2, dtype=F32) / MLA_ROPE)
    z48 = jnp.zeros((48,), F32)
    return jnp.concatenate([z48, -inv, z48, inv]).reshape(1, LANES)


def kernel(x, positions, norm_mix_g, norm_mlp_g, final_g, mla_w_in, mla_q_norm_g, mla_kv_norm_g, mla_w_uq,
           mla_w_ukv, mla_w_o, mlstm_w_in, mlstm_conv_w, mlstm_conv_b, mlstm_i_b, mlstm_f_b,
           mlstm_head_norm_g, mlstm_w_o, mlp_w1, mlp_w2):
    batch, seq, d = x.shape
    m = batch * seq
    depth = norm_mix_g.shape[0]
    tm = min(1024, seq)
    tq = min(256, seq)
    lc = min(256, seq)
    assert seq % tm == 0 and seq % tq == 0 and tm % lc == 0 and tm % CONV_HALO == 0

    h = x.reshape(m, d)
    cosp, sinp = _rope_tables(positions.reshape(m, 1), _rope_inv_pattern(), min(2048, seq))
    q_scale = (MLA_QK ** -0.5) * math.log2(math.e)
    row = lambda v: v.reshape(1, -1)
    hq = MLSTM_HEADS * MLSTM_DK
    hv = MLSTM_HEADS * MLSTM_DV

    for layer in range(depth):
        j = layer // 2
        g_mix = row(norm_mix_g[layer])
        if layer % 2 == 0:
            win, wuq, wuk, wuv = _mla_weights(mla_w_in[j], mla_w_uq[j], mla_w_ukv[j])
            q, kt, v = _mla_pre(h, g_mix, win, row(mla_q_norm_g[j]), row(mla_kv_norm_g[j]), wuq, wuk, wuv,
                               cosp, sinp, tm, q_scale)
            mixed = _attention(q, kt, v, batch, seq, tq)
            w_o = mla_w_o[j].astype(BF16)
        else:
            w = mlstm_w_in[j]
            wqk = w[:, :2 * hq].astype(BF16)
            wv = w[:, 2 * hq:2 * hq + hv].astype(BF16)
            wog = w[:, 2 * hq + hv:2 * hq + 2 * hv].astype(BF16)
            wg = w[:, 2 * hq + 2 * hv:].T.astype(BF16)
            gb = jnp.concatenate([mlstm_i_b[j], mlstm_f_b[j]]).reshape(2 * MLSTM_HEADS, 1)
            q, kt, v, og, a2, f2 = _mlstm_pre(h, g_mix, wqk, wv, wog, wg, mlstm_conv_w[j], row(mlstm_conv_b[j]), gb,
                                              row(mlstm_head_norm_g[j]), tm, seq, lc)
            mixed = _mlstm_chunk(q, kt, v, og, a2, f2, batch, seq, lc, 2 if batch % 2 == 0 else 1)
            w_o = mlstm_w_o[j].astype(BF16)
        h = _post(mixed, w_o, h, row(norm_mlp_g[layer]), mlp_w1[layer].astype(BF16),
                  mlp_w2[layer].astype(BF16), row(final_g), tm, final=(layer == depth - 1))
    return h.reshape(batch, seq, d)
```

```python
import functools
import math

import jax
import jax.numpy as jnp
from jax import lax
from jax.experimental import pallas as pl
from jax.experimental.pallas import tpu as pltpu

EPS = 1e-6
ROPE_THETA = 10000.0
GATE_SOFTCAP = 15.0

MLA_HEADS = 16
MLA_NOPE = 64
MLA_ROPE = 32
MLA_V = 64
MLA_QK = MLA_NOPE + MLA_ROPE
MLA_Q_RANK = 512
MLA_KV_RANK = 256

MLSTM_HEADS = 8
MLSTM_DK = 64
MLSTM_DV = 128
MLSTM_CONV = 4

LANES = 128
HEAD_BLOCK = LANES
KV_BLOCK = LANES
ATTN_ROWS = 32
ATTN_PAIRS_PER_STEP = 2
CONV_HALO = 16
VMEM_LIMIT = 56 * 1024 * 1024
LOG2E = math.log2(math.e)

BF16 = jnp.bfloat16
F32 = jnp.float32


def _rms(x, g):
    return x * lax.rsqrt(jnp.mean(x * x, axis=-1, keepdims=True) + EPS) * g


def _dot(a, b):
    return jnp.dot(a, b, preferred_element_type=F32)


def _sigmoid(x):
    return 1.0 / (1.0 + jnp.exp2(x * (-LOG2E)))


def _params(*sem):
    return pltpu.CompilerParams(dimension_semantics=sem, vmem_limit_bytes=VMEM_LIMIT)


def _const_spec(shape):
    return pl.BlockSpec(shape, lambda *_: (0,) * len(shape), pipeline_mode=pl.Buffered(1))


def _rope_kernel(pos_ref, inv_ref, cos_ref, sin_ref):
    ang = pos_ref[...].astype(F32) * inv_ref[...]
    cos_ref[...] = jnp.cos(ang)
    sin_ref[...] = jnp.sin(ang)


def _rope_tables(pos_col, inv_pat, tr):
    m = pos_col.shape[0]
    return pl.pallas_call(
        _rope_kernel,
        out_shape=(jax.ShapeDtypeStruct((m, LANES), F32),) * 2,
        grid=(m // tr,),
        in_specs=[pl.BlockSpec((tr, 1), lambda i: (i, 0)), _const_spec((1, LANES))],
        out_specs=(pl.BlockSpec((tr, LANES), lambda i: (i, 0)),) * 2,
        compiler_params=_params("parallel"),
        name="rope_tables",
    )(pos_col, inv_pat)


def _mla_pre_kernel(h_ref, g_ref, win_ref, gq_ref, gkv_ref, wuq_ref, wukt_ref, wuv_ref,
                    cos_ref, sin_ref, q_ref, kt_ref, v_ref, *, q_scale):
    u = _rms(h_ref[...], g_ref[...]).astype(BF16)
    proj = _dot(u, win_ref[...])
    cq = _rms(proj[:, :MLA_Q_RANK], gq_ref[...]).astype(BF16)
    ckv = _rms(proj[:, MLA_Q_RANK:MLA_Q_RANK + MLA_KV_RANK], gkv_ref[...]).astype(BF16)
    kr = proj[:, MLA_Q_RANK + MLA_KV_RANK:]
    c = cos_ref[...]
    s = sin_ref[...]
    half = HEAD_BLOCK // 2
    krt = (kr * c + pltpu.roll(kr, half, axis=1) * s).T
    q = _dot(cq, wuq_ref[...])
    knt = lax.dot_general(wukt_ref[...], ckv, (((1,), (1,)), ((), ())), preferred_element_type=F32)
    cs = c * q_scale
    ss = s * q_scale
    zpad = jnp.zeros((HEAD_BLOCK - MLA_NOPE - MLA_ROPE, krt.shape[1]), F32)
    for h in range(MLA_HEADS):
        sl = slice(h * HEAD_BLOCK, (h + 1) * HEAD_BLOCK)
        qh = q[:, sl]
        q_ref[:, sl] = (qh * cs + pltpu.roll(qh, half, axis=1) * ss).astype(BF16)
        nh = knt[h * MLA_NOPE:(h + 1) * MLA_NOPE, :]
        kt_ref[sl, :] = jnp.concatenate([nh[:48], krt[48:64], nh[48:], zpad, krt[112:]], axis=0).astype(BF16)
    v_ref[...] = _dot(ckv, wuv_ref[...]).astype(BF16)


def _mla_pre(h, g, win, gq, gkv, wuq, wuk, wuv, cosp, sinp, tm, q_scale):
    m, d = h.shape
    hq = MLA_HEADS * HEAD_BLOCK
    hv = MLA_HEADS * MLA_V
    row = lambda i: (i, 0)
    return pl.pallas_call(
        functools.partial(_mla_pre_kernel, q_scale=q_scale),
        out_shape=(jax.ShapeDtypeStruct((m, hq), BF16), jax.ShapeDtypeStruct((hq, m), BF16),
                   jax.ShapeDtypeStruct((m, hv), BF16)),
        grid=(m // tm,),
        in_specs=[pl.BlockSpec((tm, d), row), _const_spec(g.shape), _const_spec(win.shape),
                  _const_spec(gq.shape), _const_spec(gkv.shape), _const_spec(wuq.shape),
                  _const_spec(wuk.shape), _const_spec(wuv.shape),
                  pl.BlockSpec((tm, LANES), row), pl.BlockSpec((tm, LANES), row)],
        out_specs=(pl.BlockSpec((tm, hq), row), pl.BlockSpec((hq, tm), lambda i: (0, i)),
                   pl.BlockSpec((tm, hv), row)),
        compiler_params=_params("parallel"),
        name="mla_pre",
    )(h, g, win, gq, gkv, wuq, wuk, wuv, cosp, sinp)


def _attn_kernel(q_ref, kt_ref, v_ref, o_ref, kbd_ref, vbd_ref, s_ref, p_ref, *, tq):
    seq = q_ref.shape[0]
    nkb = seq // KV_BLOCK
    pw = 2 * KV_BLOCK
    npair = kbd_ref.shape[0]
    nq = seq // tq

    vw = 2 * MLA_V
    lane_v = lax.broadcasted_iota(jnp.int32, (KV_BLOCK, vw), 1)
    first_head = lane_v < MLA_V

    @pl.when((pl.program_id(0) == 0) & (pl.program_id(1) == 0))
    def _():
        kbd_ref[...] = jnp.zeros_like(kbd_ref)
        for pp in range(npair):
            for j in range(nkb):
                vbd_ref[pp, j * pw:j * pw + KV_BLOCK, vw:] = jnp.where(first_head, 1.0, 0.0).astype(BF16)
                vbd_ref[pp, j * pw + KV_BLOCK:(j + 1) * pw, vw:] = jnp.where(first_head, 0.0, 1.0).astype(BF16)

    def build(pp, j):
        ks = slice(j * KV_BLOCK, (j + 1) * KV_BLOCK)
        f0 = pp * 2 * HEAD_BLOCK
        kbd_ref[pp, :HEAD_BLOCK, j * pw:j * pw + KV_BLOCK] = kt_ref[f0:f0 + HEAD_BLOCK, ks]
        kbd_ref[pp, HEAD_BLOCK:, j * pw + KV_BLOCK:(j + 1) * pw] = kt_ref[f0 + HEAD_BLOCK:f0 + 2 * HEAD_BLOCK, ks]
        vb = v_ref[ks, pp * vw:(pp + 1) * vw]
        vbd_ref[pp, j * pw:j * pw + KV_BLOCK, :vw] = jnp.where(first_head, vb, jnp.zeros_like(vb))
        vbd_ref[pp, j * pw + KV_BLOCK:(j + 1) * pw, :vw] = jnp.where(first_head, jnp.zeros_like(vb), vb)

    row = lax.broadcasted_iota(jnp.int32, (tq, pw), 0)
    col = lax.broadcasted_iota(jnp.int32, (tq, pw), 1)
    rel = row - jnp.where(col < KV_BLOCK, col, col - KV_BLOCK)

    def scores(t):
        pp, qi = divmod(t, nq)
        q0 = qi * tq
        for j in range(q0 // KV_BLOCK, (q0 + tq) // KV_BLOCK):
            build(pp, j)
        for j in range((q0 + tq) // KV_BLOCK):
            r0 = max(0, j * KV_BLOCK - q0)
            blk = _dot(q_ref[q0 + r0:q0 + tq, pp * pw:(pp + 1) * pw],
                       kbd_ref[pp, :, j * pw:(j + 1) * pw])
            if (j + 1) * KV_BLOCK - 1 > q0:
                blk = jnp.where(rel[r0:] >= j * KV_BLOCK - q0, blk, -jnp.inf)
            s_ref[t % 2, r0:, j * pw:(j + 1) * pw] = blk

    def head_rows(x, fn):
        return jnp.concatenate([jnp.broadcast_to(fn(x[:, hh * KV_BLOCK:(hh + 1) * KV_BLOCK], axis=-1, keepdims=True),
                                                 (x.shape[0], KV_BLOCK)) for hh in range(2)], axis=1)

    def softmax(t):
        q0 = (t % nq) * tq
        slot = t % 2
        for r in range(tq // ATTN_ROWS):
            rows = slice(r * ATTN_ROWS, (r + 1) * ATTN_ROWS)
            nbr = (q0 + (r + 1) * ATTN_ROWS - 1) // KV_BLOCK + 1
            mx = functools.reduce(jnp.maximum, [s_ref[slot, rows, j * pw:(j + 1) * pw] for j in range(nbr)])
            m = head_rows(mx, jnp.max)
            for j in range(nbr):
                p_ref[slot, rows, j * pw:(j + 1) * pw] = jnp.exp2(s_ref[slot, rows, j * pw:(j + 1) * pw] - m).astype(BF16)

    def values(t):
        pp, qi = divmod(t, nq)
        q0 = qi * tq
        for g in range(tq // KV_BLOCK):
            rows = slice(g * KV_BLOCK, (g + 1) * KV_BLOCK)
            nbg = q0 // KV_BLOCK + g + 1
            o = _dot(p_ref[t % 2, rows, :nbg * pw], vbd_ref[pp, :nbg * pw, :])
            o_ref[q0 + g * KV_BLOCK:q0 + (g + 1) * KV_BLOCK, pp * vw:(pp + 1) * vw] = (
                o[:, :vw] / o[:, vw:]).astype(o_ref.dtype)

    ntile = npair * nq
    scores(0)
    for t in range(ntile):
        if t + 1 < ntile:
            scores(t + 1)
        softmax(t)
        if t >= 1:
            values(t - 1)
    values(ntile - 1)


def _attention(q, kt, v, batch, seq, tq):
    m = q.shape[0]
    npair = ATTN_PAIRS_PER_STEP
    steps = MLA_HEADS // (2 * npair)
    nkb = seq // KV_BLOCK
    blk = lambda b, p: (b, p)
    return pl.pallas_call(
        functools.partial(_attn_kernel, tq=tq),
        out_shape=jax.ShapeDtypeStruct((m, MLA_HEADS * MLA_V), BF16),
        grid=(batch, steps),
        in_specs=[pl.BlockSpec((seq, npair * 2 * HEAD_BLOCK), blk),
                  pl.BlockSpec((npair * 2 * HEAD_BLOCK, seq), lambda b, p: (p, b)),
                  pl.BlockSpec((seq, npair * 2 * MLA_V), blk)],
        out_specs=pl.BlockSpec((seq, npair * 2 * MLA_V), blk),
        scratch_shapes=[pltpu.VMEM((npair, 2 * HEAD_BLOCK, nkb * 2 * KV_BLOCK), BF16),
                        pltpu.VMEM((npair, nkb * 2 * KV_BLOCK, 4 * MLA_V), BF16),
                        pltpu.VMEM((2, tq, nkb * 2 * KV_BLOCK), F32),
                        pltpu.VMEM((2, tq, nkb * 2 * KV_BLOCK), BF16)],
        compiler_params=_params("arbitrary", "arbitrary"),
        name="mla_attention",
    )(q, kt, v)


def _post_kernel(a_ref, wo_ref, h_ref, g_ref, w1_ref, w2_ref, gf_ref, o_ref, *, hidden_chunk, final):
    h1 = h_ref[...] + _dot(a_ref[...], wo_ref[...])
    u = _rms(h1, g_ref[...]).astype(BF16)
    acc = h1
    hidden = w1_ref.shape[1]
    for c in range(hidden // hidden_chunk):
        sl = slice(c * hidden_chunk, (c + 1) * hidden_chunk)
        a = jnp.maximum(_dot(u, w1_ref[:, sl]), 0.0)
        acc = acc + _dot((a * a).astype(BF16), w2_ref[sl, :])
    if final:
        acc = _rms(acc, gf_ref[...])
    o_ref[...] = acc


def _post(a, wo, h, g, w1, w2, gf, tm, final):
    m, d = h.shape
    row = lambda i: (i, 0)
    return pl.pallas_call(
        functools.partial(_post_kernel, hidden_chunk=1024, final=final),
        out_shape=jax.ShapeDtypeStruct((m, d), F32),
        grid=(m // tm,),
        in_specs=[pl.BlockSpec((tm, a.shape[1]), row), _const_spec(wo.shape), pl.BlockSpec((tm, d), row),
                  _const_spec(g.shape), _const_spec(w1.shape), _const_spec(w2.shape), _const_spec(gf.shape)],
        out_specs=pl.BlockSpec((tm, d), row),
        compiler_params=_params("parallel"),
        name="post_mlp",
    )(a, wo, h, g, w1, w2, gf)


def _split3(x):
    x1 = x.astype(BF16)
    r1 = x - x1.astype(F32)
    x2 = r1.astype(BF16)
    x3 = (r1 - x2.astype(F32)).astype(BF16)
    return x1, x2, x3


def _mlstm_pre_kernel(h_ref, halo_ref, g_ref, wqk_ref, wv_ref, wog_ref, wg_ref, cw_ref, cb_ref, gb_ref, hg_ref,
                      q_ref, kt_ref, v_ref, og_ref, a2_ref, f2_ref, conv_ref, *, tiles_per_seq, lc):
    tm = h_ref.shape[0]
    g = g_ref[...]
    u = _rms(h_ref[...], g).astype(BF16)
    first = pl.program_id(0) % tiles_per_seq == 0
    uh = jnp.where(first, 0.0, _rms(halo_ref[...], g)).astype(BF16)
    qk = _dot(jnp.concatenate([uh, u], axis=0), wqk_ref[...])
    nslab = qk.shape[1] // LANES
    for c in range(nslab):
        conv_ref[c] = qk[:, c * LANES:(c + 1) * LANES]
    gates = lax.dot_general(wg_ref[...], u, (((1,), (1,)), ((), ())), preferred_element_type=F32) + gb_ref[...]
    gates = GATE_SOFTCAP * jnp.tanh(gates * (1.0 / GATE_SOFTCAP))
    z = gates[MLSTM_HEADS:]
    lf = -(jnp.maximum(-z, 0.0) + jnp.log1p(jnp.exp(-jnp.abs(z))))
    f2_ref[...] = lf * LOG2E
    lf_pieces = _split3(lf)
    o_pre = _dot(u, wog_ref[...])
    v_ref[...] = _dot(u, wv_ref[...]).astype(BF16)
    cw = cw_ref[...]
    cb = cb_ref[...]
    hq = MLSTM_HEADS * MLSTM_DK
    for c in range(nslab):
        sl = slice(c * LANES, (c + 1) * LANES)
        y = cb[:, sl]
        for j in range(MLSTM_CONV):
            off = CONV_HALO - (MLSTM_CONV - 1) + j
            y = y + conv_ref[c, off:off + tm, :] * cw[j:j + 1, sl]
        y = y * _sigmoid(y)
        if c * LANES < hq:
            q_ref[:, sl] = y.astype(BF16)
        else:
            kt_ref[c * LANES - hq:(c + 1) * LANES - hq, :] = (y * (MLSTM_DK ** -0.5)).T.astype(BF16)
    og_ref[...] = (_sigmoid(o_pre) * hg_ref[...]).astype(BF16)
    upper = (lax.broadcasted_iota(jnp.int32, (lc, lc), 0) <= lax.broadcasted_iota(jnp.int32, (lc, lc), 1))
    upper = jnp.where(upper, 1.0, 0.0).astype(BF16)
    for c in range(tm // lc):
        seg = slice(c * lc, (c + 1) * lc)
        b = sum(_dot(piece[:, seg], upper) for piece in lf_pieces)
        a2_ref[:, seg] = (gates[:MLSTM_HEADS, seg] - b) * LOG2E


def _mlstm_pre(h, g, wqk, wv, wog, wg, cw, cb, gb, hg, tm, seq, lc):
    m, d = h.shape
    hq = MLSTM_HEADS * MLSTM_DK
    hv = MLSTM_HEADS * MLSTM_DV
    row = lambda i: (i, 0)
    colblk = lambda i: (0, i)
    halo = lambda i: (jnp.maximum(i * (tm // CONV_HALO) - 1, 0), 0)
    return pl.pallas_call(
        functools.partial(_mlstm_pre_kernel, tiles_per_seq=seq // tm, lc=lc),
        out_shape=(jax.ShapeDtypeStruct((m, hq), BF16), jax.ShapeDtypeStruct((hq, m), BF16),
                   jax.ShapeDtypeStruct((m, hv), BF16), jax.ShapeDtypeStruct((m, hv), BF16),
                   jax.ShapeDtypeStruct((MLSTM_HEADS, m), F32), jax.ShapeDtypeStruct((MLSTM_HEADS, m), F32)),
        grid=(m // tm,),
        in_specs=[pl.BlockSpec((tm, d), row), pl.BlockSpec((CONV_HALO, d), halo), _const_spec(g.shape),
                  _const_spec(wqk.shape), _const_spec(wv.shape), _const_spec(wog.shape), _const_spec(wg.shape),
                  _const_spec(cw.shape), _const_spec(cb.shape), _const_spec(gb.shape), _const_spec(hg.shape)],
        out_specs=(pl.BlockSpec((tm, hq), row), pl.BlockSpec((hq, tm), colblk), pl.BlockSpec((tm, hv), row),
                   pl.BlockSpec((tm, hv), row), pl.BlockSpec((MLSTM_HEADS, tm), colblk),
                   pl.BlockSpec((MLSTM_HEADS, tm), colblk)),
        scratch_shapes=[pltpu.VMEM((2 * hq // LANES, CONV_HALO + tm, LANES), F32)],
        compiler_params=_params("parallel"),
        name="mlstm_pre",
    )(h, h, g, wqk, wv, wog, wg, cw, cb, gb, hg)


def _mlstm_chunk_kernel(*refs, nseq):
    n_in = 6
    ins = [refs[n_in * r:n_in * (r + 1)] for r in range(nseq)]
    o_ref, ct_ref, m_ref = refs[n_in * nseq:]
    lc = ins[0][0].shape[0]
    pair = 2 * MLSTM_DK
    pairs = MLSTM_HEADS // 2

    @pl.when(pl.program_id(1) == 0)
    def _():
        ct_ref[...] = jnp.zeros_like(ct_ref)
        m_ref[...] = jnp.zeros_like(m_ref)

    causal = lax.broadcasted_iota(jnp.int32, (lc, lc), 0) >= lax.broadcasted_iota(jnp.int32, (lc, lc), 1)
    lane = lax.broadcasted_iota(jnp.int32, (lc, pair), 1)
    ones = jnp.ones((lc, MLSTM_DV), BF16)
    top = lax.broadcasted_iota(jnp.int32, (pair, MLSTM_DV), 0) < MLSTM_DK
    top2 = jnp.concatenate([top, top], axis=1)

    def state_independent(r):
        q_ref, kt_ref, v_ref, _, a2_ref, f2_ref = ins[r]
        a2 = a2_ref[...]
        f2 = f2_ref[...]
        mrows = slice(r * MLSTM_HEADS, (r + 1) * MLSTM_HEADS)
        m2_prev = m_ref[mrows, :]
        a_max = jnp.max(a2, axis=1, keepdims=True)
        m_last = jnp.maximum(m2_prev, a_max)
        m_ref[mrows, :] = jnp.sum(f2, axis=1, keepdims=True) + m_last
        a_st = jnp.exp2(m2_prev - m_last)
        w_st = jnp.exp2(a2 - jnp.maximum(m2_prev[:, 0:1], a_max)).astype(BF16)
        qms, raws, vaugs, ct_bs = [], [], [], []
        for p in range(pairs):
            qb = q_ref[:, p * pair:(p + 1) * pair]
            ktb = kt_ref[p * pair:(p + 1) * pair, :]
            ct = ct_ref[r * pairs + p]
            ct_bs.append(ct.astype(BF16))
            upd = []
            for hh in range(2):
                h = 2 * p + hh
                qm = jnp.where((lane < MLSTM_DK) if hh == 0 else (lane >= MLSTM_DK), qb, jnp.zeros_like(qb))
                vaug = jnp.concatenate([v_ref[:, h * MLSTM_DV:(h + 1) * MLSTM_DV], ones], axis=1)
                qms.append(qm)
                vaugs.append(vaug)
                raws.append(_dot(qm, ktb))
                upd.append(_dot(ktb * w_st[h:h + 1, :], vaug))
            decay = jnp.where(top, a_st[2 * p:2 * p + 1, :], a_st[2 * p + 1:2 * p + 2, :])
            ct_ref[r * pairs + p] = jnp.concatenate([decay, decay], axis=1) * ct + jnp.where(top2, upd[0], upd[1])
        return a2, f2, m2_prev, qms, raws, vaugs, ct_bs

    for r in range(nseq):
        a2, f2, m2_prev, qms, raws, vaugs, ct_bs = state_independent(r)
        og_ref = ins[r][3]
        for h in range(MLSTM_HEADS):
            qm = qms[h]
            d = jnp.where(causal, a2[h:h + 1, :], -jnp.inf)
            m_h = jnp.broadcast_to(m2_prev[h:h + 1, :], (lc, MLSTM_DV))
            m_col = jnp.maximum(m_h, jnp.max(d, axis=1, keepdims=True))
            b_col = jnp.sum(jnp.where(causal, f2[h:h + 1, :], 0.0), axis=1, keepdims=True)
            s = raws[h] * jnp.exp2(d - jnp.concatenate([m_col] * (lc // MLSTM_DV), axis=1))
            q_inter = (qm.astype(F32) * jnp.exp2(m_h - m_col)).astype(BF16)
            nd = _dot(jnp.concatenate([s.astype(BF16), q_inter], axis=1),
                      jnp.concatenate([vaugs[h], ct_bs[h // 2]], axis=0))
            den = jnp.maximum(jnp.abs(nd[:, MLSTM_DV:]), jnp.exp2(-(b_col + m_col)))
            hv = nd[:, :MLSTM_DV] / den
            hn = hv * lax.rsqrt(jnp.mean(hv * hv, axis=-1, keepdims=True) + EPS)
            vsl = slice(h * MLSTM_DV, (h + 1) * MLSTM_DV)
            o_ref[r, :, vsl] = og_ref[:, vsl] * hn.astype(BF16)


def _mlstm_chunk(q, kt, v, og, a2, f2, batch, seq, lc, nseq):
    m = q.shape[0]
    hq = MLSTM_HEADS * MLSTM_DK
    hv = MLSTM_HEADS * MLSTM_DV
    nc = seq // lc
    in_specs = []
    for r in range(nseq):
        row = lambda b, c, r=r: ((b * nseq + r) * nc + c, 0)
        colblk = lambda b, c, r=r: (0, (b * nseq + r) * nc + c)
        in_specs += [pl.BlockSpec((lc, hq), row), pl.BlockSpec((hq, lc), colblk), pl.BlockSpec((lc, hv), row),
                     pl.BlockSpec((lc, hv), row), pl.BlockSpec((MLSTM_HEADS, lc), colblk),
                     pl.BlockSpec((MLSTM_HEADS, lc), colblk)]
    out = pl.pallas_call(
        functools.partial(_mlstm_chunk_kernel, nseq=nseq),
        out_shape=jax.ShapeDtypeStruct((batch // nseq, nseq, seq, hv), BF16),
        grid=(batch // nseq, nc),
        in_specs=in_specs,
        out_specs=pl.BlockSpec((None, nseq, lc, hv), lambda b, c: (b, 0, c, 0)),
        scratch_shapes=[pltpu.VMEM((nseq * MLSTM_HEADS // 2, 2 * MLSTM_DK, 2 * MLSTM_DV), F32),
                        pltpu.VMEM((nseq * MLSTM_HEADS, LANES), F32)],
        compiler_params=_params("parallel", "arbitrary"),
        name="mlstm_chunk",
    )(*([q, kt, v, og, a2, f2] * nseq))
    return out.reshape(m, hv)


def _head_block(nope, x1, x2):
    z = jnp.zeros(nope.shape[:-1] + (HEAD_BLOCK - MLA_NOPE - MLA_ROPE,), nope.dtype)
    return jnp.concatenate([nope[..., :48], x1, nope[..., 48:], z, x2], axis=-1)


def _mla_weights(w_in, w_uq, w_ukv):
    d = w_in.shape[0]
    half = MLA_ROPE // 2
    kr = w_in[:, MLA_Q_RANK + MLA_KV_RANK:]
    kr_blk = _head_block(jnp.zeros((d, MLA_NOPE), w_in.dtype), kr[:, :half], kr[:, half:])
    win = jnp.concatenate([w_in[:, :MLA_Q_RANK + MLA_KV_RANK], kr_blk], axis=1).astype(BF16)
    uq = w_uq.reshape(MLA_Q_RANK, MLA_HEADS, MLA_QK)
    wuq = _head_block(uq[..., :MLA_NOPE], uq[..., MLA_NOPE:MLA_NOPE + half], uq[..., MLA_NOPE + half:])
    wuq = wuq.reshape(MLA_Q_RANK, MLA_HEADS * HEAD_BLOCK).astype(BF16)
    ukv = w_ukv.reshape(MLA_KV_RANK, MLA_HEADS, MLA_NOPE + MLA_V)
    wuk = ukv[..., :MLA_NOPE].reshape(MLA_KV_RANK, MLA_HEADS * MLA_NOPE).T.astype(BF16)
    wuv = ukv[..., MLA_NOPE:].reshape(MLA_KV_RANK, MLA_HEADS * MLA_V).astype(BF16)
    return win, wuq, wuk, wuv


def _rope_inv_pattern():
    inv = ROPE_THETA ** (-jnp.arange(0, MLA_ROPE, 2, dtype=F32) / MLA_ROPE)
    z48 = jnp.zeros((48,), F32)
    return jnp.concatenate([z48, -inv, z48, inv]).reshape(1, LANES)


def kernel(x, positions, norm_mix_g, norm_mlp_g, final_g, mla_w_in, mla_q_norm_g, mla_kv_norm_g, mla_w_uq,
           mla_w_ukv, mla_w_o, mlstm_w_in, mlstm_conv_w, mlstm_conv_b, mlstm_i_b, mlstm_f_b,
           mlstm_head_norm_g, mlstm_w_o, mlp_w1, mlp_w2):
    batch, seq, d = x.shape
    m = batch * seq
    depth = norm_mix_g.shape[0]
    tm = min(1024, seq)
    tq = min(256, seq)
    lc = min(256, seq)
    assert seq % tm == 0 and seq % tq == 0 and tm % lc == 0 and tm % CONV_HALO == 0

    h = x.reshape(m, d)
    cosp, sinp = _rope_tables(positions.reshape(m, 1), _rope_inv_pattern(), min(2048, seq))
    q_scale = (MLA_QK ** -0.5) * math.log2(math.e)
    row = lambda v: v.reshape(1, -1)
    hq = MLSTM_HEADS * MLSTM_DK
    hv = MLSTM_HEADS * MLSTM_DV

    for layer in range(depth):
        j = layer // 2
        g_mix = row(norm_mix_g[layer])
        if layer % 2 == 0:
            win, wuq, wuk, wuv = _mla_weights(mla_w_in[j], mla_w_uq[j], mla_w_ukv[j])
            q, kt, v = _mla_pre(h, g_mix, win, row(mla_q_norm_g[j]), row(mla_kv_norm_g[j]), wuq, wuk, wuv,
                               cosp, sinp, tm, q_scale)
            mixed = _attention(q, kt, v, batch, seq, tq)
            w_o = mla_w_o[j].astype(BF16)
        else:
            w = mlstm_w_in[j]
            wqk = w[:, :2 * hq].astype(BF16)
            wv = w[:, 2 * hq:2 * hq + hv].astype(BF16)
            wog = w[:, 2 * hq + hv:2 * hq + 2 * hv].astype(BF16)
            wg = w[:, 2 * hq + 2 * hv:].T.astype(BF16)
            gb = jnp.concatenate([mlstm_i_b[j], mlstm_f_b[j]]).reshape(2 * MLSTM_HEADS, 1)
            q, kt, v, og, a2, f2 = _mlstm_pre(h, g_mix, wqk, wv, wog, wg, mlstm_conv_w[j], row(mlstm_conv_b[j]), gb,
                                              row(mlstm_head_norm_g[j]), tm, seq, lc)
            mixed = _mlstm_chunk(q, kt, v, og, a2, f2, batch, seq, lc, 2 if batch % 2 == 0 else 1)
            w_o = mlstm_w_o[j].astype(BF16)
        h = _post(mixed, w_o, h, row(norm_mlp_g[layer]), mlp_w1[layer].astype(BF16),
                  mlp_w2[layer].astype(BF16), row(final_g), tm, final=(layer == depth - 1))
    return h.reshape(batch, seq, d)
```

```python
import functools
import math

import jax
import jax.numpy as jnp
from jax import lax
from jax.experimental import pallas as pl
from jax.experimental.pallas import tpu as pltpu

EPS = 1e-6
ROPE_THETA = 10000.0
GATE_SOFTCAP = 15.0

MLA_HEADS = 16
MLA_NOPE = 64
MLA_ROPE = 32
MLA_V = 64
MLA_QK = MLA_NOPE + MLA_ROPE
MLA_Q_RANK = 512
MLA_KV_RANK = 256

MLSTM_HEADS = 8
MLSTM_DK = 64
MLSTM_DV = 128
MLSTM_CONV = 4

LANES = 128
HEAD_BLOCK = LANES
NOPE_LO = HEAD_BLOCK // 2 - MLA_ROPE // 2
KV_BLOCK = LANES
ATTN_ROWS = 32
ATTN_PAIRS_PER_STEP = 2
CONV_HALO = 16
VMEM_LIMIT = 56 * 1024 * 1024
LOG2E = math.log2(math.e)

BF16 = jnp.bfloat16
F32 = jnp.float32


def _rms(x, g):
    return x * lax.rsqrt(jnp.mean(x * x, axis=-1, keepdims=True) + EPS) * g


def _dot(a, b):
    return jnp.dot(a, b, preferred_element_type=F32)


def _sigmoid(x):
    return 1.0 / (1.0 + jnp.exp2(x * (-LOG2E)))


def _params(*sem):
    return pltpu.CompilerParams(dimension_semantics=sem, vmem_limit_bytes=VMEM_LIMIT)


def _const_spec(shape):
    return pl.BlockSpec(shape, lambda *_: (0,) * len(shape), pipeline_mode=pl.Buffered(1))


def _rope_kernel(pos_ref, inv_ref, cos_ref, sin_ref):
    ang = pos_ref[...].astype(F32) * inv_ref[...]
    cos_ref[...] = jnp.cos(ang)
    sin_ref[...] = jnp.sin(ang)


def _rope_tables(pos_col, inv_pat, tr):
    m = pos_col.shape[0]
    return pl.pallas_call(
        _rope_kernel,
        out_shape=(jax.ShapeDtypeStruct((m, LANES), F32),) * 2,
        grid=(m // tr,),
        in_specs=[pl.BlockSpec((tr, 1), lambda i: (i, 0)), _const_spec((1, LANES))],
        out_specs=(pl.BlockSpec((tr, LANES), lambda i: (i, 0)),) * 2,
        compiler_params=_params("parallel"),
        name="rope_tables",
    )(pos_col, inv_pat)


def _mla_pre_kernel(h_ref, g_ref, win_ref, gq_ref, gkv_ref, wuq_ref, wukt_ref, wuv_ref,
                    cos_ref, sin_ref, q_ref, kt_ref, v_ref, *, q_scale):
    u = _rms(h_ref[...], g_ref[...]).astype(BF16)
    proj = _dot(u, win_ref[...])
    cq = _rms(proj[:, :MLA_Q_RANK], gq_ref[...]).astype(BF16)
    ckv = _rms(proj[:, MLA_Q_RANK:MLA_Q_RANK + MLA_KV_RANK], gkv_ref[...]).astype(BF16)
    kr = proj[:, MLA_Q_RANK + MLA_KV_RANK:]
    c = cos_ref[...]
    s = sin_ref[...]
    half = HEAD_BLOCK // 2
    first = lax.broadcasted_iota(jnp.int32, kr.shape, 1) < half
    krt = (kr * c + pltpu.roll(kr, half, axis=1) * jnp.where(first, -s, s)).T
    cs = c * q_scale
    ss = s * q_scale
    for p in range(MLA_HEADS // 2):
        qp = _dot(cq, wuq_ref[:, p * 2 * HEAD_BLOCK:(p + 1) * 2 * HEAD_BLOCK])
        qa = qp[:, :HEAD_BLOCK]
        qb = qp[:, HEAD_BLOCK:]
        q_ref[:, p * 2 * HEAD_BLOCK:p * 2 * HEAD_BLOCK + HEAD_BLOCK] = (qa * cs - qb * ss).astype(BF16)
        q_ref[:, p * 2 * HEAD_BLOCK + HEAD_BLOCK:(p + 1) * 2 * HEAD_BLOCK] = (qb * cs + qa * ss).astype(BF16)
    knt = lax.dot_general(wukt_ref[...], ckv, (((1,), (1,)), ((), ())), preferred_element_type=F32)
    zpad = jnp.zeros((HEAD_BLOCK - MLA_NOPE - MLA_ROPE, krt.shape[1]), F32)
    x2_lo = HEAD_BLOCK // 2 + NOPE_LO
    for h in range(MLA_HEADS):
        nh = knt[h * MLA_NOPE:(h + 1) * MLA_NOPE, :]
        kt_ref[h * HEAD_BLOCK:(h + 1) * HEAD_BLOCK, :] = jnp.concatenate(
            [nh[:NOPE_LO], krt[NOPE_LO:HEAD_BLOCK // 2], nh[NOPE_LO:], zpad, krt[x2_lo:]], axis=0).astype(BF16)
    v_ref[...] = _dot(ckv, wuv_ref[...]).astype(BF16)


def _mla_pre(h, g, win, gq, gkv, wuq, wuk, wuv, cosp, sinp, tm, q_scale):
    m, d = h.shape
    hq = MLA_HEADS * HEAD_BLOCK
    hv = MLA_HEADS * MLA_V
    row = lambda i: (i, 0)
    return pl.pallas_call(
        functools.partial(_mla_pre_kernel, q_scale=q_scale),
        out_shape=(jax.ShapeDtypeStruct((m, hq), BF16), jax.ShapeDtypeStruct((hq, m), BF16),
                   jax.ShapeDtypeStruct((m, hv), BF16)),
        grid=(m // tm,),
        in_specs=[pl.BlockSpec((tm, d), row), _const_spec(g.shape), _const_spec(win.shape),
                  _const_spec(gq.shape), _const_spec(gkv.shape), _const_spec(wuq.shape),
                  _const_spec(wuk.shape), _const_spec(wuv.shape),
                  pl.BlockSpec((tm, LANES), row), pl.BlockSpec((tm, LANES), row)],
        out_specs=(pl.BlockSpec((tm, hq), row), pl.BlockSpec((hq, tm), lambda i: (0, i)),
                   pl.BlockSpec((tm, hv), row)),
        compiler_params=_params("parallel"),
        name="mla_pre",
    )(h, g, win, gq, gkv, wuq, wuk, wuv, cosp, sinp)


def _attn_kernel(q_ref, kt_ref, v_ref, o_ref, kbd_ref, vbd_ref, s_ref, p_ref, *, tq):
    seq = q_ref.shape[0]
    nkb = seq // KV_BLOCK
    pw = 2 * KV_BLOCK
    npair = kbd_ref.shape[0]
    nq = seq // tq

    vw = 2 * MLA_V
    lane_v = lax.broadcasted_iota(jnp.int32, (KV_BLOCK, vw), 1)
    first_head = lane_v < MLA_V

    @pl.when((pl.program_id(0) == 0) & (pl.program_id(1) == 0))
    def _():
        kbd_ref[...] = jnp.zeros_like(kbd_ref)
        for pp in range(npair):
            for j in range(nkb):
                vbd_ref[pp, j * pw:j * pw + KV_BLOCK, vw:] = jnp.where(first_head, 1.0, 0.0).astype(BF16)
                vbd_ref[pp, j * pw + KV_BLOCK:(j + 1) * pw, vw:] = jnp.where(first_head, 0.0, 1.0).astype(BF16)

    def build(pp, j):
        ks = slice(j * KV_BLOCK, (j + 1) * KV_BLOCK)
        f0 = pp * 2 * HEAD_BLOCK
        hh = HEAD_BLOCK // 2
        for head in range(2):
            cols = slice(j * pw + head * KV_BLOCK, j * pw + (head + 1) * KV_BLOCK)
            for part in range(2):
                src = f0 + head * HEAD_BLOCK + part * hh
                dst = part * HEAD_BLOCK + head * hh
                kbd_ref[pp, dst:dst + hh, cols] = kt_ref[src:src + hh, ks]
        vb = v_ref[ks, pp * vw:(pp + 1) * vw]
        vbd_ref[pp, j * pw:j * pw + KV_BLOCK, :vw] = jnp.where(first_head, vb, jnp.zeros_like(vb))
        vbd_ref[pp, j * pw + KV_BLOCK:(j + 1) * pw, :vw] = jnp.where(first_head, jnp.zeros_like(vb), vb)

    row = lax.broadcasted_iota(jnp.int32, (tq, pw), 0)
    col = lax.broadcasted_iota(jnp.int32, (tq, pw), 1)
    rel = row - jnp.where(col < KV_BLOCK, col, col - KV_BLOCK)

    def scores(t):
        pp, qi = divmod(t, nq)
        q0 = qi * tq
        for j in range(q0 // KV_BLOCK, (q0 + tq) // KV_BLOCK):
            build(pp, j)
        for j in range((q0 + tq) // KV_BLOCK):
            r0 = max(0, j * KV_BLOCK - q0)
            blk = _dot(q_ref[q0 + r0:q0 + tq, pp * pw:(pp + 1) * pw],
                       kbd_ref[pp, :, j * pw:(j + 1) * pw])
            if (j + 1) * KV_BLOCK - 1 > q0:
                blk = jnp.where(rel[r0:] >= j * KV_BLOCK - q0, blk, -jnp.inf)
            s_ref[t % 2, r0:, j * pw:(j + 1) * pw] = blk

    def head_rows(x, fn):
        return jnp.concatenate([jnp.broadcast_to(fn(x[:, hh * KV_BLOCK:(hh + 1) * KV_BLOCK], axis=-1, keepdims=True),
                                                 (x.shape[0], KV_BLOCK)) for hh in range(2)], axis=1)

    def softmax(t):
        q0 = (t % nq) * tq
        slot = t % 2
        for r in range(tq // ATTN_ROWS):
            rows = slice(r * ATTN_ROWS, (r + 1) * ATTN_ROWS)
            nbr = (q0 + (r + 1) * ATTN_ROWS - 1) // KV_BLOCK + 1
            mx = functools.reduce(jnp.maximum, [s_ref[slot, rows, j * pw:(j + 1) * pw] for j in range(nbr)])
            m = head_rows(mx, jnp.max)
            for j in range(nbr):
                p_ref[slot, rows, j * pw:(j + 1) * pw] = jnp.exp2(s_ref[slot, rows, j * pw:(j + 1) * pw] - m).astype(BF16)

    def values(t):
        pp, qi = divmod(t, nq)
        q0 = qi * tq
        for g in range(tq // KV_BLOCK):
            rows = slice(g * KV_BLOCK, (g + 1) * KV_BLOCK)
            nbg = q0 // KV_BLOCK + g + 1
            o = _dot(p_ref[t % 2, rows, :nbg * pw], vbd_ref[pp, :nbg * pw, :])
            o_ref[q0 + g * KV_BLOCK:q0 + (g + 1) * KV_BLOCK, pp * vw:(pp + 1) * vw] = (
                o[:, :vw] / o[:, vw:]).astype(o_ref.dtype)

    ntile = npair * nq
    scores(0)
    for t in range(ntile):
        if t + 1 < ntile:
            scores(t + 1)
        softmax(t)
        if t >= 1:
            values(t - 1)
    values(ntile - 1)


def _attention(q, kt, v, batch, seq, tq):
    m = q.shape[0]
    npair = ATTN_PAIRS_PER_STEP
    steps = MLA_HEADS // (2 * npair)
    nkb = seq // KV_BLOCK
    blk = lambda b, p: (b, p)
    return pl.pallas_call(
        functools.partial(_attn_kernel, tq=tq),
        out_shape=jax.ShapeDtypeStruct((m, MLA_HEADS * MLA_V), BF16),
        grid=(batch, steps),
        in_specs=[pl.BlockSpec((seq, npair * 2 * HEAD_BLOCK), blk),
                  pl.BlockSpec((npair * 2 * HEAD_BLOCK, seq), lambda b, p: (p, b)),
                  pl.BlockSpec((seq, npair * 2 * MLA_V), blk)],
        out_specs=pl.BlockSpec((seq, npair * 2 * MLA_V), blk),
        scratch_shapes=[pltpu.VMEM((npair, 2 * HEAD_BLOCK, nkb * 2 * KV_BLOCK), BF16),
                        pltpu.VMEM((npair, nkb * 2 * KV_BLOCK, 4 * MLA_V), BF16),
                        pltpu.VMEM((2, tq, nkb * 2 * KV_BLOCK), F32),
                        pltpu.VMEM((2, tq, nkb * 2 * KV_BLOCK), BF16)],
        compiler_params=_params("arbitrary", "arbitrary"),
        name="mla_attention",
    )(q, kt, v)


def _post_kernel(a_ref, wo_ref, h_ref, g_ref, w1_ref, w2_ref, gf_ref, o_ref, *, hidden_chunk, final):
    h1 = h_ref[...] + _dot(a_ref[...], wo_ref[...])
    u = _rms(h1, g_ref[...]).astype(BF16)
    acc = h1
    hidden = w1_ref.shape[1]
    for c in range(hidden // hidden_chunk):
        sl = slice(c * hidden_chunk, (c + 1) * hidden_chunk)
        a = jnp.maximum(_dot(u, w1_ref[:, sl]), 0.0)
        acc = acc + _dot((a * a).astype(BF16), w2_ref[sl, :])
    if final:
        acc = _rms(acc, gf_ref[...])
    o_ref[...] = acc


def _post(a, wo, h, g, w1, w2, gf, tm, final):
    m, d = h.shape
    row = lambda i: (i, 0)
    return pl.pallas_call(
        functools.partial(_post_kernel, hidden_chunk=1024, final=final),
        out_shape=jax.ShapeDtypeStruct((m, d), F32),
        grid=(m // tm,),
        in_specs=[pl.BlockSpec((tm, a.shape[1]), row), _const_spec(wo.shape), pl.BlockSpec((tm, d), row),
                  _const_spec(g.shape), _const_spec(w1.shape), _const_spec(w2.shape), _const_spec(gf.shape)],
        out_specs=pl.BlockSpec((tm, d), row),
        compiler_params=_params("parallel"),
        name="post_mlp",
    )(a, wo, h, g, w1, w2, gf)


def _split3(x):
    x1 = x.astype(BF16)
    r1 = x - x1.astype(F32)
    x2 = r1.astype(BF16)
    x3 = (r1 - x2.astype(F32)).astype(BF16)
    return x1, x2, x3


def _mlstm_pre_kernel(h_ref, halo_ref, g_ref, wqk_ref, wv_ref, wog_ref, wg_ref, cw_ref, cb_ref, gb_ref, hg_ref,
                      q_ref, kt_ref, v_ref, og_ref, a2_ref, f2_ref, conv_ref, *, tiles_per_seq, lc):
    tm = h_ref.shape[0]
    g = g_ref[...]
    u = _rms(h_ref[...], g).astype(BF16)
    first = pl.program_id(0) % tiles_per_seq == 0
    uh = jnp.where(first, 0.0, _rms(halo_ref[...], g)).astype(BF16)
    qk = _dot(jnp.concatenate([uh, u], axis=0), wqk_ref[...])
    nslab = qk.shape[1] // LANES
    for c in range(nslab):
        conv_ref[c] = qk[:, c * LANES:(c + 1) * LANES]
    gates = lax.dot_general(wg_ref[...], u, (((1,), (1,)), ((), ())), preferred_element_type=F32) + gb_ref[...]
    gates = GATE_SOFTCAP * jnp.tanh(gates * (1.0 / GATE_SOFTCAP))
    z = gates[MLSTM_HEADS:]
    lf = -(jnp.maximum(-z, 0.0) + jnp.log1p(jnp.exp(-jnp.abs(z))))
    f2_ref[...] = lf * LOG2E
    lf_pieces = _split3(lf)
    o_pre = _dot(u, wog_ref[...])
    v_ref[...] = _dot(u, wv_ref[...]).astype(BF16)
    cw = cw_ref[...]
    cb = cb_ref[...]
    hq = MLSTM_HEADS * MLSTM_DK
    for c in range(nslab):
        sl = slice(c * LANES, (c + 1) * LANES)
        y = cb[:, sl]
        for j in range(MLSTM_CONV):
            off = CONV_HALO - (MLSTM_CONV - 1) + j
            y = y + conv_ref[c, off:off + tm, :] * cw[j:j + 1, sl]
        y = y * _sigmoid(y)
        if c * LANES < hq:
            q_ref[:, sl] = y.astype(BF16)
        else:
            kt_ref[c * LANES - hq:(c + 1) * LANES - hq, :] = (y * (MLSTM_DK ** -0.5)).T.astype(BF16)
    og_ref[...] = (_sigmoid(o_pre) * hg_ref[...]).astype(BF16)
    upper = (lax.broadcasted_iota(jnp.int32, (lc, lc), 0) <= lax.broadcasted_iota(jnp.int32, (lc, lc), 1))
    upper = jnp.where(upper, 1.0, 0.0).astype(BF16)
    for c in range(tm // lc):
        seg = slice(c * lc, (c + 1) * lc)
        b = sum(_dot(piece[:, seg], upper) for piece in lf_pieces)
        a2_ref[:, seg] = (gates[:MLSTM_HEADS, seg] - b) * LOG2E


def _mlstm_pre(h, g, wqk, wv, wog, wg, cw, cb, gb, hg, tm, seq, lc):
    m, d = h.shape
    hq = MLSTM_HEADS * MLSTM_DK
    hv = MLSTM_HEADS * MLSTM_DV
    row = lambda i: (i, 0)
    colblk = lambda i: (0, i)
    halo = lambda i: (jnp.maximum(i * (tm // CONV_HALO) - 1, 0), 0)
    return pl.pallas_call(
        functools.partial(_mlstm_pre_kernel, tiles_per_seq=seq // tm, lc=lc),
        out_shape=(jax.ShapeDtypeStruct((m, hq), BF16), jax.ShapeDtypeStruct((hq, m), BF16),
                   jax.ShapeDtypeStruct((m, hv), BF16), jax.ShapeDtypeStruct((m, hv), BF16),
                   jax.ShapeDtypeStruct((MLSTM_HEADS, m), F32), jax.ShapeDtypeStruct((MLSTM_HEADS, m), F32)),
        grid=(m // tm,),
        in_specs=[pl.BlockSpec((tm, d), row), pl.BlockSpec((CONV_HALO, d), halo), _const_spec(g.shape),
                  _const_spec(wqk.shape), _const_spec(wv.shape), _const_spec(wog.shape), _const_spec(wg.shape),
                  _const_spec(cw.shape), _const_spec(cb.shape), _const_spec(gb.shape), _const_spec(hg.shape)],
        out_specs=(pl.BlockSpec((tm, hq), row), pl.BlockSpec((hq, tm), colblk), pl.BlockSpec((tm, hv), row),
                   pl.BlockSpec((tm, hv), row), pl.BlockSpec((MLSTM_HEADS, tm), colblk),
                   pl.BlockSpec((MLSTM_HEADS, tm), colblk)),
        scratch_shapes=[pltpu.VMEM((2 * hq // LANES, CONV_HALO + tm, LANES), F32)],
        compiler_params=_params("parallel"),
        name="mlstm_pre",
    )(h, h, g, wqk, wv, wog, wg, cw, cb, gb, hg)


def _mlstm_chunk_kernel(*refs, nseq):
    n_in = 6
    ins = [refs[n_in * r:n_in * (r + 1)] for r in range(nseq)]
    o_ref, ct_ref, m_ref = refs[n_in * nseq:]
    lc = ins[0][0].shape[0]
    pair = 2 * MLSTM_DK
    pairs = MLSTM_HEADS // 2

    @pl.when(pl.program_id(1) == 0)
    def _():
        ct_ref[...] = jnp.zeros_like(ct_ref)
        m_ref[...] = jnp.zeros_like(m_ref)

    causal = lax.broadcasted_iota(jnp.int32, (lc, lc), 0) >= lax.broadcasted_iota(jnp.int32, (lc, lc), 1)
    lane = lax.broadcasted_iota(jnp.int32, (lc, pair), 1)
    ones = jnp.ones((lc, MLSTM_DV), BF16)
    top = lax.broadcasted_iota(jnp.int32, (pair, MLSTM_DV), 0) < MLSTM_DK
    top2 = jnp.concatenate([top, top], axis=1)

    def state_independent(r):
        q_ref, kt_ref, v_ref, _, a2_ref, f2_ref = ins[r]
        a2 = a2_ref[...]
        f2 = f2_ref[...]
        mrows = slice(r * MLSTM_HEADS, (r + 1) * MLSTM_HEADS)
        m2_prev = m_ref[mrows, :]
        a_max = jnp.max(a2, axis=1, keepdims=True)
        m_last = jnp.maximum(m2_prev, a_max)
        m_ref[mrows, :] = jnp.sum(f2, axis=1, keepdims=True) + m_last
        a_st = jnp.exp2(m2_prev - m_last)
        w_st = jnp.exp2(a2 - jnp.maximum(m2_prev[:, 0:1], a_max)).astype(BF16)
        qms, raws, vaugs, ct_bs = [], [], [], []
        for p in range(pairs):
            qb = q_ref[:, p * pair:(p + 1) * pair]
            ktb = kt_ref[p * pair:(p + 1) * pair, :]
            ct = ct_ref[r * pairs + p]
            ct_bs.append(ct.astype(BF16))
            upd = []
            for hh in range(2):
                h = 2 * p + hh
                qm = jnp.where((lane < MLSTM_DK) if hh == 0 else (lane >= MLSTM_DK), qb, jnp.zeros_like(qb))
                vaug = jnp.concatenate([v_ref[:, h * MLSTM_DV:(h + 1) * MLSTM_DV], ones], axis=1)
                qms.append(qm)
                vaugs.append(vaug)
                raws.append(_dot(qm, ktb))
                upd.append(_dot(ktb * w_st[h:h + 1, :], vaug))
            decay = jnp.where(top, a_st[2 * p:2 * p + 1, :], a_st[2 * p + 1:2 * p + 2, :])
            ct_ref[r * pairs + p] = jnp.concatenate([decay, decay], axis=1) * ct + jnp.where(top2, upd[0], upd[1])
        return a2, f2, m2_prev, qms, raws, vaugs, ct_bs

    for r in range(nseq):
        a2, f2, m2_prev, qms, raws, vaugs, ct_bs = state_independent(r)
        og_ref = ins[r][3]
        for h in range(MLSTM_HEADS):
            qm = qms[h]
            d = jnp.where(causal, a2[h:h + 1, :], -jnp.inf)
            m_h = jnp.broadcast_to(m2_prev[h:h + 1, :], (lc, MLSTM_DV))
            m_col = jnp.maximum(m_h, jnp.max(d, axis=1, keepdims=True))
            b_col = jnp.sum(jnp.where(causal, f2[h:h + 1, :], 0.0), axis=1, keepdims=True)
            s = raws[h] * jnp.exp2(d - jnp.concatenate([m_col] * (lc // MLSTM_DV), axis=1))
            q_inter = (qm.astype(F32) * jnp.exp2(m_h - m_col)).astype(BF16)
            nd = _dot(jnp.concatenate([s.astype(BF16), q_inter], axis=1),
                      jnp.concatenate([vaugs[h], ct_bs[h // 2]], axis=0))
            den = jnp.maximum(jnp.abs(nd[:, MLSTM_DV:]), jnp.exp2(-(b_col + m_col)))
            hv = nd[:, :MLSTM_DV] / den
            hn = hv * lax.rsqrt(jnp.mean(hv * hv, axis=-1, keepdims=True) + EPS)
            vsl = slice(h * MLSTM_DV, (h + 1) * MLSTM_DV)
            o_ref[r, :, vsl] = og_ref[:, vsl] * hn.astype(BF16)


def _mlstm_chunk(q, kt, v, og, a2, f2, batch, seq, lc, nseq):
    m = q.shape[0]
    hq = MLSTM_HEADS * MLSTM_DK
    hv = MLSTM_HEADS * MLSTM_DV
    nc = seq // lc
    in_specs = []
    for r in range(nseq):
        row = lambda b, c, r=r: ((b * nseq + r) * nc + c, 0)
        colblk = lambda b, c, r=r: (0, (b * nseq + r) * nc + c)
        in_specs += [pl.BlockSpec((lc, hq), row), pl.BlockSpec((hq, lc), colblk), pl.BlockSpec((lc, hv), row),
                     pl.BlockSpec((lc, hv), row), pl.BlockSpec((MLSTM_HEADS, lc), colblk),
                     pl.BlockSpec((MLSTM_HEADS, lc), colblk)]
    out = pl.pallas_call(
        functools.partial(_mlstm_chunk_kernel, nseq=nseq),
        out_shape=jax.ShapeDtypeStruct((batch // nseq, nseq, seq, hv), BF16),
        grid=(batch // nseq, nc),
        in_specs=in_specs,
        out_specs=pl.BlockSpec((None, nseq, lc, hv), lambda b, c: (b, 0, c, 0)),
        scratch_shapes=[pltpu.VMEM((nseq * MLSTM_HEADS // 2, 2 * MLSTM_DK, 2 * MLSTM_DV), F32),
                        pltpu.VMEM((nseq * MLSTM_HEADS, LANES), F32)],
        compiler_params=_params("parallel", "arbitrary"),
        name="mlstm_chunk",
    )(*([q, kt, v, og, a2, f2] * nseq))
    return out.reshape(m, hv)


def _head_halves(nope, x1, x2):
    z = jnp.zeros(nope.shape[:-1] + (HEAD_BLOCK - MLA_NOPE - MLA_ROPE,), nope.dtype)
    return (jnp.concatenate([nope[..., :NOPE_LO], x1], axis=-1),
            jnp.concatenate([nope[..., NOPE_LO:], z, x2], axis=-1))


def _mla_weights(w_in, w_uq, w_ukv):
    d = w_in.shape[0]
    half = MLA_ROPE // 2
    kr = w_in[:, MLA_Q_RANK + MLA_KV_RANK:]
    kr_blk = jnp.concatenate(_head_halves(jnp.zeros((d, MLA_NOPE), w_in.dtype), kr[:, :half], kr[:, half:]), axis=1)
    win = jnp.concatenate([w_in[:, :MLA_Q_RANK + MLA_KV_RANK], kr_blk], axis=1).astype(BF16)
    uq = w_uq.reshape(MLA_Q_RANK, MLA_HEADS, MLA_QK)
    qa, qb = _head_halves(uq[..., :MLA_NOPE], uq[..., MLA_NOPE:MLA_NOPE + half], uq[..., MLA_NOPE + half:])
    pair_cols = lambda t: t.reshape(MLA_Q_RANK, MLA_HEADS // 2, HEAD_BLOCK)
    wuq = jnp.concatenate([pair_cols(qa), pair_cols(qb)], axis=-1)
    wuq = wuq.reshape(MLA_Q_RANK, MLA_HEADS * HEAD_BLOCK).astype(BF16)
    ukv = w_ukv.reshape(MLA_KV_RANK, MLA_HEADS, MLA_NOPE + MLA_V)
    wuk = ukv[..., :MLA_NOPE].reshape(MLA_KV_RANK, MLA_HEADS * MLA_NOPE).T.astype(BF16)
    wuv = ukv[..., MLA_NOPE:].reshape(MLA_KV_RANK, MLA_HEADS * MLA_V).astype(BF16)
    return win, wuq, wuk, wuv


def _rope_inv_pattern():
    inv = ROPE_THETA ** (-jnp.arange(0, MLA_ROPE, 2, dtype=F32) / MLA_ROPE)
    zeros = jnp.zeros((NOPE_LO,), F32)
    return jnp.concatenate([zeros, inv, zeros, inv]).reshape(1, LANES)


def kernel(x, positions, norm_mix_g, norm_mlp_g, final_g, mla_w_in, mla_q_norm_g, mla_kv_norm_g, mla_w_uq,
           mla_w_ukv, mla_w_o, mlstm_w_in, mlstm_conv_w, mlstm_conv_b, mlstm_i_b, mlstm_f_b,
           mlstm_head_norm_g, mlstm_w_o, mlp_w1, mlp_w2):
    batch, seq, d = x.shape
    m = batch * seq
    depth = norm_mix_g.shape[0]
    tm = min(1024, seq)
    tq = min(256, seq)
    lc = min(256, seq)
    assert seq % tm == 0 and seq % tq == 0 and tm % lc == 0 and tm % CONV_HALO == 0

    h = x.reshape(m, d)
    cosp, sinp = _rope_tables(positions.reshape(m, 1), _rope_inv_pattern(), min(2048, seq))
    q_scale = (MLA_QK ** -0.5) * math.log2(math.e)
    row = lambda v: v.reshape(1, -1)
    hq = MLSTM_HEADS * MLSTM_DK
    hv = MLSTM_HEADS * MLSTM_DV

    for layer in range(depth):
        j = layer // 2
        g_mix = row(norm_mix_g[layer])
        if layer % 2 == 0:
            win, wuq, wuk, wuv = _mla_weights(mla_w_in[j], mla_w_uq[j], mla_w_ukv[j])
            q, kt, v = _mla_pre(h, g_mix, win, row(mla_q_norm_g[j]), row(mla_kv_norm_g[j]), wuq, wuk, wuv,
                               cosp, sinp, tm, q_scale)
            mixed = _attention(q, kt, v, batch, seq, tq)
            w_o = mla_w_o[j].astype(BF16)
        else:
            w = mlstm_w_in[j]
            wqk = w[:, :2 * hq].astype(BF16)
            wv = w[:, 2 * hq:2 * hq + hv].astype(BF16)
            wog = w[:, 2 * hq + hv:2 * hq + 2 * hv].astype(BF16)
            wg = w[:, 2 * hq + 2 * hv:].T.astype(BF16)
            gb = jnp.concatenate([mlstm_i_b[j], mlstm_f_b[j]]).reshape(2 * MLSTM_HEADS, 1)
            q, kt, v, og, a2, f2 = _mlstm_pre(h, g_mix, wqk, wv, wog, wg, mlstm_conv_w[j], row(mlstm_conv_b[j]), gb,
                                              row(mlstm_head_norm_g[j]), tm, seq, lc)
            mixed = _mlstm_chunk(q, kt, v, og, a2, f2, batch, seq, lc, 2 if batch % 2 == 0 else 1)
            w_o = mlstm_w_o[j].astype(BF16)
        h = _post(mixed, w_o, h, row(norm_mlp_g[layer]), mlp_w1[layer].astype(BF16),
                  mlp_w2[layer].astype(BF16), row(final_g), tm, final=(layer == depth - 1))
    return h.reshape(batch, seq, d)
```

```python
import functools
import math

import jax
import jax.numpy as jnp
from jax import lax
from jax.experimental import pallas as pl
from jax.experimental.pallas import tpu as pltpu

EPS = 1e-6
ROPE_THETA = 10000.0
GATE_SOFTCAP = 15.0

MLA_HEADS = 16
MLA_NOPE = 64
MLA_ROPE = 32
MLA_V = 64
MLA_QK = MLA_NOPE + MLA_ROPE
MLA_Q_RANK = 512
MLA_KV_RANK = 256

MLSTM_HEADS = 8
MLSTM_DK = 64
MLSTM_DV = 128
MLSTM_CONV = 4

LANES = 128
HEAD_BLOCK = LANES
NOPE_LO = HEAD_BLOCK // 2 - MLA_ROPE // 2
KV_BLOCK = LANES
ATTN_ROWS = 32
ATTN_PAIRS_PER_STEP = 2
CONV_HALO = 16
VMEM_LIMIT = 56 * 1024 * 1024
LOG2E = math.log2(math.e)

BF16 = jnp.bfloat16
F32 = jnp.float32


def _rms(x, g):
    return x * lax.rsqrt(jnp.mean(x * x, axis=-1, keepdims=True) + EPS) * g


def _dot(a, b):
    return jnp.dot(a, b, preferred_element_type=F32)


def _sigmoid(x):
    return 1.0 / (1.0 + jnp.exp2(x * (-LOG2E)))


def _params(*sem):
    return pltpu.CompilerParams(dimension_semantics=sem, vmem_limit_bytes=VMEM_LIMIT)


def _const_spec(shape):
    return pl.BlockSpec(shape, lambda *_: (0,) * len(shape), pipeline_mode=pl.Buffered(1))


def _rope_kernel(pos_ref, inv_ref, cos_ref, sin_ref):
    rows = pos_ref.shape[0] // 2
    half = LANES // 2
    pos = pos_ref[...].astype(F32)
    low = lax.broadcasted_iota(jnp.int32, (rows, LANES), 1) < half
    ang = jnp.where(low, pos[:rows], pos[rows:]) * inv_ref[...]
    for ref, val in ((cos_ref, jnp.cos(ang)), (sin_ref, jnp.sin(ang))):
        swapped = pltpu.roll(val, half, axis=1)
        ref[:rows] = jnp.where(low, val, swapped)
        ref[rows:] = jnp.where(low, swapped, val)


def _rope_tables(pos_col, inv_pat, tr):
    m = pos_col.shape[0]
    return pl.pallas_call(
        _rope_kernel,
        out_shape=(jax.ShapeDtypeStruct((m, LANES), F32),) * 2,
        grid=(m // tr,),
        in_specs=[pl.BlockSpec((tr, 1), lambda i: (i, 0)), _const_spec((1, LANES))],
        out_specs=(pl.BlockSpec((tr, LANES), lambda i: (i, 0)),) * 2,
        compiler_params=_params("parallel"),
        name="rope_tables",
    )(pos_col, inv_pat)


def _mla_pre_kernel(h_ref, g_ref, win_ref, gq_ref, gkv_ref, wuq_ref, wukt_ref, wuv_ref,
                    cos_ref, sin_ref, q_ref, kt_ref, v_ref, *, q_scale):
    u = _rms(h_ref[...], g_ref[...]).astype(BF16)
    proj = _dot(u, win_ref[...])
    cq = _rms(proj[:, :MLA_Q_RANK], gq_ref[...]).astype(BF16)
    ckv = _rms(proj[:, MLA_Q_RANK:MLA_Q_RANK + MLA_KV_RANK], gkv_ref[...]).astype(BF16)
    kr = proj[:, MLA_Q_RANK + MLA_KV_RANK:]
    c = cos_ref[...]
    s = sin_ref[...]
    half = HEAD_BLOCK // 2
    first = lax.broadcasted_iota(jnp.int32, kr.shape, 1) < half
    krt = (kr * c + pltpu.roll(kr, half, axis=1) * jnp.where(first, -s, s)).T
    cs = c * q_scale
    ss = s * q_scale
    for p in range(MLA_HEADS // 2):
        qp = _dot(cq, wuq_ref[:, p * 2 * HEAD_BLOCK:(p + 1) * 2 * HEAD_BLOCK])
        qa = qp[:, :HEAD_BLOCK]
        qb = qp[:, HEAD_BLOCK:]
        q_ref[:, p * 2 * HEAD_BLOCK:p * 2 * HEAD_BLOCK + HEAD_BLOCK] = (qa * cs - qb * ss).astype(BF16)
        q_ref[:, p * 2 * HEAD_BLOCK + HEAD_BLOCK:(p + 1) * 2 * HEAD_BLOCK] = (qb * cs + qa * ss).astype(BF16)
    knt = lax.dot_general(wukt_ref[...], ckv, (((1,), (1,)), ((), ())), preferred_element_type=F32)
    zpad = jnp.zeros((HEAD_BLOCK - MLA_NOPE - MLA_ROPE, krt.shape[1]), F32)
    x2_lo = HEAD_BLOCK // 2 + NOPE_LO
    for h in range(MLA_HEADS):
        nh = knt[h * MLA_NOPE:(h + 1) * MLA_NOPE, :]
        kt_ref[h * HEAD_BLOCK:(h + 1) * HEAD_BLOCK, :] = jnp.concatenate(
            [nh[:NOPE_LO], krt[NOPE_LO:HEAD_BLOCK // 2], nh[NOPE_LO:], zpad, krt[x2_lo:]], axis=0).astype(BF16)
    v_ref[...] = _dot(ckv, wuv_ref[...]).astype(BF16)


def _mla_pre(h, g, win, gq, gkv, wuq, wuk, wuv, cosp, sinp, tm, q_scale):
    m, d = h.shape
    hq = MLA_HEADS * HEAD_BLOCK
    hv = MLA_HEADS * MLA_V
    row = lambda i: (i, 0)
    return pl.pallas_call(
        functools.partial(_mla_pre_kernel, q_scale=q_scale),
        out_shape=(jax.ShapeDtypeStruct((m, hq), BF16), jax.ShapeDtypeStruct((hq, m), BF16),
                   jax.ShapeDtypeStruct((m, hv), BF16)),
        grid=(m // tm,),
        in_specs=[pl.BlockSpec((tm, d), row), _const_spec(g.shape), _const_spec(win.shape),
                  _const_spec(gq.shape), _const_spec(gkv.shape), _const_spec(wuq.shape),
                  _const_spec(wuk.shape), _const_spec(wuv.shape),
                  pl.BlockSpec((tm, LANES), row), pl.BlockSpec((tm, LANES), row)],
        out_specs=(pl.BlockSpec((tm, hq), row), pl.BlockSpec((hq, tm), lambda i: (0, i)),
                   pl.BlockSpec((tm, hv), row)),
        compiler_params=_params("parallel"),
        name="mla_pre",
    )(h, g, win, gq, gkv, wuq, wuk, wuv, cosp, sinp)


def _attn_kernel(q_ref, kt_ref, v_ref, o_ref, kbd_ref, vbd_ref, s_ref, p_ref, *, tq):
    seq = q_ref.shape[0]
    nkb = seq // KV_BLOCK
    pw = 2 * KV_BLOCK
    npair = kbd_ref.shape[0]
    nq = seq // tq

    vw = 2 * MLA_V
    lane_v = lax.broadcasted_iota(jnp.int32, (KV_BLOCK, vw), 1)
    first_head = lane_v < MLA_V

    @pl.when((pl.program_id(0) == 0) & (pl.program_id(1) == 0))
    def _():
        kbd_ref[...] = jnp.zeros_like(kbd_ref)
        for pp in range(npair):
            for j in range(nkb):
                vbd_ref[pp, j * pw:j * pw + KV_BLOCK, vw:] = jnp.where(first_head, 1.0, 0.0).astype(BF16)
                vbd_ref[pp, j * pw + KV_BLOCK:(j + 1) * pw, vw:] = jnp.where(first_head, 0.0, 1.0).astype(BF16)

    def build(pp, j):
        ks = slice(j * KV_BLOCK, (j + 1) * KV_BLOCK)
        f0 = pp * 2 * HEAD_BLOCK
        hh = HEAD_BLOCK // 2
        for head in range(2):
            cols = slice(j * pw + head * KV_BLOCK, j * pw + (head + 1) * KV_BLOCK)
            for part in range(2):
                src = f0 + head * HEAD_BLOCK + part * hh
                dst = part * HEAD_BLOCK + head * hh
                kbd_ref[pp, dst:dst + hh, cols] = kt_ref[src:src + hh, ks]
        vb = v_ref[ks, pp * vw:(pp + 1) * vw]
        vbd_ref[pp, j * pw:j * pw + KV_BLOCK, :vw] = jnp.where(first_head, vb, jnp.zeros_like(vb))
        vbd_ref[pp, j * pw + KV_BLOCK:(j + 1) * pw, :vw] = jnp.where(first_head, jnp.zeros_like(vb), vb)

    row = lax.broadcasted_iota(jnp.int32, (tq, pw), 0)
    col = lax.broadcasted_iota(jnp.int32, (tq, pw), 1)
    rel = row - jnp.where(col < KV_BLOCK, col, col - KV_BLOCK)

    def scores(t):
        pp, qi = divmod(t, nq)
        q0 = qi * tq
        for j in range(q0 // KV_BLOCK, (q0 + tq) // KV_BLOCK):
            build(pp, j)
        for j in range((q0 + tq) // KV_BLOCK):
            r0 = max(0, j * KV_BLOCK - q0)
            blk = _dot(q_ref[q0 + r0:q0 + tq, pp * pw:(pp + 1) * pw],
                       kbd_ref[pp, :, j * pw:(j + 1) * pw])
            if (j + 1) * KV_BLOCK - 1 > q0:
                blk = jnp.where(rel[r0:] >= j * KV_BLOCK - q0, blk, -jnp.inf)
            s_ref[t % 2, r0:, j * pw:(j + 1) * pw] = blk

    def head_rows(x, fn):
        return jnp.concatenate([jnp.broadcast_to(fn(x[:, hh * KV_BLOCK:(hh + 1) * KV_BLOCK], axis=-1, keepdims=True),
                                                 (x.shape[0], KV_BLOCK)) for hh in range(2)], axis=1)

    def softmax(t):
        q0 = (t % nq) * tq
        slot = t % 2
        for r in range(tq // ATTN_ROWS):
            rows = slice(r * ATTN_ROWS, (r + 1) * ATTN_ROWS)
            nbr = (q0 + (r + 1) * ATTN_ROWS - 1) // KV_BLOCK + 1
            mx = functools.reduce(jnp.maximum, [s_ref[slot, rows, j * pw:(j + 1) * pw] for j in range(nbr)])
            m = head_rows(mx, jnp.max)
            for j in range(nbr):
                p_ref[slot, rows, j * pw:(j + 1) * pw] = jnp.exp2(s_ref[slot, rows, j * pw:(j + 1) * pw] - m).astype(BF16)

    def values(t):
        pp, qi = divmod(t, nq)
        q0 = qi * tq
        for g in range(tq // KV_BLOCK):
            rows = slice(g * KV_BLOCK, (g + 1) * KV_BLOCK)
            nbg = q0 // KV_BLOCK + g + 1
            o = _dot(p_ref[t % 2, rows, :nbg * pw], vbd_ref[pp, :nbg * pw, :])
            o_ref[q0 + g * KV_BLOCK:q0 + (g + 1) * KV_BLOCK, pp * vw:(pp + 1) * vw] = (
                o[:, :vw] / o[:, vw:]).astype(o_ref.dtype)

    ntile = npair * nq
    scores(0)
    for t in range(ntile):
        if t + 1 < ntile:
            scores(t + 1)
        softmax(t)
        if t >= 1:
            values(t - 1)
    values(ntile - 1)


def _attention(q, kt, v, batch, seq, tq):
    m = q.shape[0]
    npair = ATTN_PAIRS_PER_STEP
    steps = MLA_HEADS // (2 * npair)
    nkb = seq // KV_BLOCK
    blk = lambda b, p: (b, p)
    return pl.pallas_call(
        functools.partial(_attn_kernel, tq=tq),
        out_shape=jax.ShapeDtypeStruct((m, MLA_HEADS * MLA_V), BF16),
        grid=(batch, steps),
        in_specs=[pl.BlockSpec((seq, npair * 2 * HEAD_BLOCK), blk),
                  pl.BlockSpec((npair * 2 * HEAD_BLOCK, seq), lambda b, p: (p, b)),
                  pl.BlockSpec((seq, npair * 2 * MLA_V), blk)],
        out_specs=pl.BlockSpec((seq, npair * 2 * MLA_V), blk),
        scratch_shapes=[pltpu.VMEM((npair, 2 * HEAD_BLOCK, nkb * 2 * KV_BLOCK), BF16),
                        pltpu.VMEM((npair, nkb * 2 * KV_BLOCK, 4 * MLA_V), BF16),
                        pltpu.VMEM((2, tq, nkb * 2 * KV_BLOCK), F32),
                        pltpu.VMEM((2, tq, nkb * 2 * KV_BLOCK), BF16)],
        compiler_params=_params("arbitrary", "arbitrary"),
        name="mla_attention",
    )(q, kt, v)


def _post_kernel(a_ref, wo_ref, h_ref, g_ref, w1_ref, w2_ref, gf_ref, o_ref, *, hidden_chunk, final):
    h1 = h_ref[...] + _dot(a_ref[...], wo_ref[...])
    u = _rms(h1, g_ref[...]).astype(BF16)
    acc = h1
    hidden = w1_ref.shape[1]
    for c in range(hidden // hidden_chunk):
        sl = slice(c * hidden_chunk, (c + 1) * hidden_chunk)
        a = jnp.maximum(_dot(u, w1_ref[:, sl]), 0.0)
        acc = acc + _dot((a * a).astype(BF16), w2_ref[sl, :])
    if final:
        acc = _rms(acc, gf_ref[...])
    o_ref[...] = acc


def _post(a, wo, h, g, w1, w2, gf, tm, final):
    m, d = h.shape
    row = lambda i: (i, 0)
    return pl.pallas_call(
        functools.partial(_post_kernel, hidden_chunk=1024, final=final),
        out_shape=jax.ShapeDtypeStruct((m, d), F32),
        grid=(m // tm,),
        in_specs=[pl.BlockSpec((tm, a.shape[1]), row), _const_spec(wo.shape), pl.BlockSpec((tm, d), row),
                  _const_spec(g.shape), _const_spec(w1.shape), _const_spec(w2.shape), _const_spec(gf.shape)],
        out_specs=pl.BlockSpec((tm, d), row),
        compiler_params=_params("parallel"),
        name="post_mlp",
    )(a, wo, h, g, w1, w2, gf)


def _split3(x):
    x1 = x.astype(BF16)
    r1 = x - x1.astype(F32)
    x2 = r1.astype(BF16)
    x3 = (r1 - x2.astype(F32)).astype(BF16)
    return x1, x2, x3


def _mlstm_pre_kernel(h_ref, halo_ref, g_ref, wqk_ref, wv_ref, wog_ref, wg_ref, cw_ref, cb_ref, gb_ref, hg_ref,
                      q_ref, kt_ref, v_ref, og_ref, a2_ref, f2_ref, conv_ref, *, tiles_per_seq, lc):
    tm = h_ref.shape[0]
    g = g_ref[...]
    u = _rms(h_ref[...], g).astype(BF16)
    first = pl.program_id(0) % tiles_per_seq == 0
    uh = jnp.where(first, 0.0, _rms(halo_ref[...], g)).astype(BF16)
    qk = _dot(jnp.concatenate([uh, u], axis=0), wqk_ref[...])
    nslab = qk.shape[1] // LANES
    for c in range(nslab):
        conv_ref[c] = qk[:, c * LANES:(c + 1) * LANES]
    gates = lax.dot_general(wg_ref[...], u, (((1,), (1,)), ((), ())), preferred_element_type=F32) + gb_ref[...]
    gates = GATE_SOFTCAP * jnp.tanh(gates * (1.0 / GATE_SOFTCAP))
    z = gates[MLSTM_HEADS:]
    lf = -(jnp.maximum(-z, 0.0) + jnp.log1p(jnp.exp(-jnp.abs(z))))
    f2_ref[...] = lf * LOG2E
    lf_pieces = _split3(lf)
    o_pre = _dot(u, wog_ref[...])
    v_ref[...] = _dot(u, wv_ref[...]).astype(BF16)
    cw = cw_ref[...]
    cb = cb_ref[...]
    hq = MLSTM_HEADS * MLSTM_DK
    for c in range(nslab):
        sl = slice(c * LANES, (c + 1) * LANES)
        y = cb[:, sl]
        for j in range(MLSTM_CONV):
            off = CONV_HALO - (MLSTM_CONV - 1) + j
            y = y + conv_ref[c, off:off + tm, :] * cw[j:j + 1, sl]
        y = y * _sigmoid(y)
        if c * LANES < hq:
            q_ref[:, sl] = y.astype(BF16)
        else:
            kt_ref[c * LANES - hq:(c + 1) * LANES - hq, :] = (y * (MLSTM_DK ** -0.5)).T.astype(BF16)
    og_ref[...] = (_sigmoid(o_pre) * hg_ref[...]).astype(BF16)
    upper = (lax.broadcasted_iota(jnp.int32, (lc, lc), 0) <= lax.broadcasted_iota(jnp.int32, (lc, lc), 1))
    upper = jnp.where(upper, 1.0, 0.0).astype(BF16)
    for c in range(tm // lc):
        seg = slice(c * lc, (c + 1) * lc)
        b = sum(_dot(piece[:, seg], upper) for piece in lf_pieces)
        a2_ref[:, seg] = (gates[:MLSTM_HEADS, seg] - b) * LOG2E


def _mlstm_pre(h, g, wqk, wv, wog, wg, cw, cb, gb, hg, tm, seq, lc):
    m, d = h.shape
    hq = MLSTM_HEADS * MLSTM_DK
    hv = MLSTM_HEADS * MLSTM_DV
    row = lambda i: (i, 0)
    colblk = lambda i: (0, i)
    halo = lambda i: (jnp.maximum(i * (tm // CONV_HALO) - 1, 0), 0)
    return pl.pallas_call(
        functools.partial(_mlstm_pre_kernel, tiles_per_seq=seq // tm, lc=lc),
        out_shape=(jax.ShapeDtypeStruct((m, hq), BF16), jax.ShapeDtypeStruct((hq, m), BF16),
                   jax.ShapeDtypeStruct((m, hv), BF16), jax.ShapeDtypeStruct((m, hv), BF16),
                   jax.ShapeDtypeStruct((MLSTM_HEADS, m), F32), jax.ShapeDtypeStruct((MLSTM_HEADS, m), F32)),
        grid=(m // tm,),
        in_specs=[pl.BlockSpec((tm, d), row), pl.BlockSpec((CONV_HALO, d), halo), _const_spec(g.shape),
                  _const_spec(wqk.shape), _const_spec(wv.shape), _const_spec(wog.shape), _const_spec(wg.shape),
                  _const_spec(cw.shape), _const_spec(cb.shape), _const_spec(gb.shape), _const_spec(hg.shape)],
        out_specs=(pl.BlockSpec((tm, hq), row), pl.BlockSpec((hq, tm), colblk), pl.BlockSpec((tm, hv), row),
                   pl.BlockSpec((tm, hv), row), pl.BlockSpec((MLSTM_HEADS, tm), colblk),
                   pl.BlockSpec((MLSTM_HEADS, tm), colblk)),
        scratch_shapes=[pltpu.VMEM((2 * hq // LANES, CONV_HALO + tm, LANES), F32)],
        compiler_params=_params("parallel"),
        name="mlstm_pre",
    )(h, h, g, wqk, wv, wog, wg, cw, cb, gb, hg)


def _mlstm_chunk_kernel(*refs, nseq):
    n_in = 6
    ins = [refs[n_in * r:n_in * (r + 1)] for r in range(nseq)]
    o_ref, ct_ref, m_ref = refs[n_in * nseq:]
    lc = ins[0][0].shape[0]
    pair = 2 * MLSTM_DK
    pairs = MLSTM_HEADS // 2

    @pl.when(pl.program_id(1) == 0)
    def _():
        ct_ref[...] = jnp.zeros_like(ct_ref)
        m_ref[...] = jnp.zeros_like(m_ref)

    causal = lax.broadcasted_iota(jnp.int32, (lc, lc), 0) >= lax.broadcasted_iota(jnp.int32, (lc, lc), 1)
    lane = lax.broadcasted_iota(jnp.int32, (lc, pair), 1)
    ones = jnp.ones((lc, MLSTM_DV), BF16)
    top = lax.broadcasted_iota(jnp.int32, (pair, MLSTM_DV), 0) < MLSTM_DK
    top2 = jnp.concatenate([top, top], axis=1)

    def state_independent(r):
        q_ref, kt_ref, v_ref, _, a2_ref, f2_ref = ins[r]
        a2 = a2_ref[...]
        f2 = f2_ref[...]
        mrows = slice(r * MLSTM_HEADS, (r + 1) * MLSTM_HEADS)
        m2_prev = m_ref[mrows, :]
        a_max = jnp.max(a2, axis=1, keepdims=True)
        m_last = jnp.maximum(m2_prev, a_max)
        m_ref[mrows, :] = jnp.sum(f2, axis=1, keepdims=True) + m_last
        a_st = jnp.exp2(m2_prev - m_last)
        w_st = jnp.exp2(a2 - jnp.maximum(m2_prev[:, 0:1], a_max)).astype(BF16)
        qms, raws, vaugs, ct_bs = [], [], [], []
        for p in range(pairs):
            qb = q_ref[:, p * pair:(p + 1) * pair]
            ktb = kt_ref[p * pair:(p + 1) * pair, :]
            ct = ct_ref[r * pairs + p]
            ct_bs.append(ct.astype(BF16))
            upd = []
            for hh in range(2):
                h = 2 * p + hh
                qm = jnp.where((lane < MLSTM_DK) if hh == 0 else (lane >= MLSTM_DK), qb, jnp.zeros_like(qb))
                vaug = jnp.concatenate([v_ref[:, h * MLSTM_DV:(h + 1) * MLSTM_DV], ones], axis=1)
                qms.append(qm)
                vaugs.append(vaug)
                raws.append(_dot(qm, ktb))
                upd.append(_dot(ktb * w_st[h:h + 1, :], vaug))
            decay = jnp.where(top, a_st[2 * p:2 * p + 1, :], a_st[2 * p + 1:2 * p + 2, :])
            ct_ref[r * pairs + p] = jnp.concatenate([decay, decay], axis=1) * ct + jnp.where(top2, upd[0], upd[1])
        return a2, f2, m2_prev, qms, raws, vaugs, ct_bs

    for r in range(nseq):
        a2, f2, m2_prev, qms, raws, vaugs, ct_bs = state_independent(r)
        og_ref = ins[r][3]
        for h in range(MLSTM_HEADS):
            qm = qms[h]
            d = jnp.where(causal, a2[h:h + 1, :], -jnp.inf)
            m_h = jnp.broadcast_to(m2_prev[h:h + 1, :], (lc, MLSTM_DV))
            m_col = jnp.maximum(m_h, jnp.max(d, axis=1, keepdims=True))
            b_col = jnp.sum(jnp.where(causal, f2[h:h + 1, :], 0.0), axis=1, keepdims=True)
            s = raws[h] * jnp.exp2(d - jnp.concatenate([m_col] * (lc // MLSTM_DV), axis=1))
            q_inter = (qm.astype(F32) * jnp.exp2(m_h - m_col)).astype(BF16)
            nd = _dot(jnp.concatenate([s.astype(BF16), q_inter], axis=1),
                      jnp.concatenate([vaugs[h], ct_bs[h // 2]], axis=0))
            den = jnp.maximum(jnp.abs(nd[:, MLSTM_DV:]), jnp.exp2(-(b_col + m_col)))
            hv = nd[:, :MLSTM_DV] / den
            hn = hv * lax.rsqrt(jnp.mean(hv * hv, axis=-1, keepdims=True) + EPS)
            vsl = slice(h * MLSTM_DV, (h + 1) * MLSTM_DV)
            o_ref[r, :, vsl] = og_ref[:, vsl] * hn.astype(BF16)


def _mlstm_chunk(q, kt, v, og, a2, f2, batch, seq, lc, nseq):
    m = q.shape[0]
    hq = MLSTM_HEADS * MLSTM_DK
    hv = MLSTM_HEADS * MLSTM_DV
    nc = seq // lc
    in_specs = []
    for r in range(nseq):
        row = lambda b, c, r=r: ((b * nseq + r) * nc + c, 0)
        colblk = lambda b, c, r=r: (0, (b * nseq + r) * nc + c)
        in_specs += [pl.BlockSpec((lc, hq), row), pl.BlockSpec((hq, lc), colblk), pl.BlockSpec((lc, hv), row),
                     pl.BlockSpec((lc, hv), row), pl.BlockSpec((MLSTM_HEADS, lc), colblk),
                     pl.BlockSpec((MLSTM_HEADS, lc), colblk)]
    out = pl.pallas_call(
        functools.partial(_mlstm_chunk_kernel, nseq=nseq),
        out_shape=jax.ShapeDtypeStruct((batch // nseq, nseq, seq, hv), BF16),
        grid=(batch // nseq, nc),
        in_specs=in_specs,
        out_specs=pl.BlockSpec((None, nseq, lc, hv), lambda b, c: (b, 0, c, 0)),
        scratch_shapes=[pltpu.VMEM((nseq * MLSTM_HEADS // 2, 2 * MLSTM_DK, 2 * MLSTM_DV), F32),
                        pltpu.VMEM((nseq * MLSTM_HEADS, LANES), F32)],
        compiler_params=_params("parallel", "arbitrary"),
        name="mlstm_chunk",
    )(*([q, kt, v, og, a2, f2] * nseq))
    return out.reshape(m, hv)


def _head_halves(nope, x1, x2):
    z = jnp.zeros(nope.shape[:-1] + (HEAD_BLOCK - MLA_NOPE - MLA_ROPE,), nope.dtype)
    return (jnp.concatenate([nope[..., :NOPE_LO], x1], axis=-1),
            jnp.concatenate([nope[..., NOPE_LO:], z, x2], axis=-1))


def _mla_weights(w_in, w_uq, w_ukv):
    d = w_in.shape[0]
    half = MLA_ROPE // 2
    kr = w_in[:, MLA_Q_RANK + MLA_KV_RANK:]
    kr_blk = jnp.concatenate(_head_halves(jnp.zeros((d, MLA_NOPE), w_in.dtype), kr[:, :half], kr[:, half:]), axis=1)
    win = jnp.concatenate([w_in[:, :MLA_Q_RANK + MLA_KV_RANK], kr_blk], axis=1).astype(BF16)
    uq = w_uq.reshape(MLA_Q_RANK, MLA_HEADS, MLA_QK)
    qa, qb = _head_halves(uq[..., :MLA_NOPE], uq[..., MLA_NOPE:MLA_NOPE + half], uq[..., MLA_NOPE + half:])
    pair_cols = lambda t: t.reshape(MLA_Q_RANK, MLA_HEADS // 2, HEAD_BLOCK)
    wuq = jnp.concatenate([pair_cols(qa), pair_cols(qb)], axis=-1)
    wuq = wuq.reshape(MLA_Q_RANK, MLA_HEADS * HEAD_BLOCK).astype(BF16)
    ukv = w_ukv.reshape(MLA_KV_RANK, MLA_HEADS, MLA_NOPE + MLA_V)
    wuk = ukv[..., :MLA_NOPE].reshape(MLA_KV_RANK, MLA_HEADS * MLA_NOPE).T.astype(BF16)
    wuv = ukv[..., MLA_NOPE:].reshape(MLA_KV_RANK, MLA_HEADS * MLA_V).astype(BF16)
    return win, wuq, wuk, wuv


def _rope_inv_pattern():
    inv = ROPE_THETA ** (-jnp.arange(0, MLA_ROPE, 2, dtype=F32) / MLA_ROPE)
    zeros = jnp.zeros((NOPE_LO,), F32)
    return jnp.concatenate([zeros, inv, zeros, inv]).reshape(1, LANES)


def kernel(x, positions, norm_mix_g, norm_mlp_g, final_g, mla_w_in, mla_q_norm_g, mla_kv_norm_g, mla_w_uq,
           mla_w_ukv, mla_w_o, mlstm_w_in, mlstm_conv_w, mlstm_conv_b, mlstm_i_b, mlstm_f_b,
           mlstm_head_norm_g, mlstm_w_o, mlp_w1, mlp_w2):
    batch, seq, d = x.shape
    m = batch * seq
    depth = norm_mix_g.shape[0]
    tm = min(1024, seq)
    tq = min(256, seq)
    lc = min(256, seq)
    assert seq % tm == 0 and seq % tq == 0 and tm % lc == 0 and tm % CONV_HALO == 0

    h = x.reshape(m, d)
    cosp, sinp = _rope_tables(positions.reshape(m, 1), _rope_inv_pattern(), min(2048, seq))
    q_scale = (MLA_QK ** -0.5) * math.log2(math.e)
    row = lambda v: v.reshape(1, -1)
    hq = MLSTM_HEADS * MLSTM_DK
    hv = MLSTM_HEADS * MLSTM_DV

    for layer in range(depth):
        j = layer // 2
        g_mix = row(norm_mix_g[layer])
        if layer % 2 == 0:
            win, wuq, wuk, wuv = _mla_weights(mla_w_in[j], mla_w_uq[j], mla_w_ukv[j])
            q, kt, v = _mla_pre(h, g_mix, win, row(mla_q_norm_g[j]), row(mla_kv_norm_g[j]), wuq, wuk, wuv,
                               cosp, sinp, tm, q_scale)
            mixed = _attention(q, kt, v, batch, seq, tq)
            w_o = mla_w_o[j].astype(BF16)
        else:
            w = mlstm_w_in[j]
            wqk = w[:, :2 * hq].astype(BF16)
            wv = w[:, 2 * hq:2 * hq + hv].astype(BF16)
            wog = w[:, 2 * hq + hv:2 * hq + 2 * hv].astype(BF16)
            wg = w[:, 2 * hq + 2 * hv:].T.astype(BF16)
            gb = jnp.concatenate([mlstm_i_b[j], mlstm_f_b[j]]).reshape(2 * MLSTM_HEADS, 1)
            q, kt, v, og, a2, f2 = _mlstm_pre(h, g_mix, wqk, wv, wog, wg, mlstm_conv_w[j], row(mlstm_conv_b[j]), gb,
                                              row(mlstm_head_norm_g[j]), tm, seq, lc)
            mixed = _mlstm_chunk(q, kt, v, og, a2, f2, batch, seq, lc, 2 if batch % 2 == 0 else 1)
            w_o = mlstm_w_o[j].astype(BF16)
        h = _post(mixed, w_o, h, row(norm_mlp_g[layer]), mlp_w1[layer].astype(BF16),
                  mlp_w2[layer].astype(BF16), row(final_g), tm, final=(layer == depth - 1))
    return h.reshape(batch, seq, d)
```
